```python
import math
import jax, jax.numpy as jnp
from jax import lax
import numpy as np

D_MODEL = 1024
BATCH = 32
SEQ = 256
DEPTH = 4
DEC_BATCH = 4
DEC_SEQ = 2048
PAST_LEN = 512

GRID_W = 64
HEAD_DIM = 128
N_Q_HEADS = 8
N_KV_HEADS = 2
Q_PER_KV = N_Q_HEADS // N_KV_HEADS
ATTN_WIDTH = N_Q_HEADS * HEAD_DIM
KV_WIDTH = N_KV_HEADS * HEAD_DIM
WINDOW = 128
BLOCK = 128
SCALE = HEAD_DIM ** -0.5
ROPE_AXIS_DIM = HEAD_DIM // 2
ROPE_THETA = 10000.0
D_RNN = D_MODEL
N_RNN_BLOCKS = 8
RNN_BLOCK_W = D_RNN // N_RNN_BLOCKS
CONV_W = 4
CONV_LEFT = 2
LRU_C = 8.0
D_FF = 4 * D_MODEL
N_MOD = 6
EPS = 1e-6
NEG = -1e30
SPLIT_Q = ATTN_WIDTH
SPLIT_K = SPLIT_Q + KV_WIDTH
SPLIT_V = SPLIT_K + KV_WIDTH
SPLIT_XR = SPLIT_V + D_RNN
SPLIT_XG = SPLIT_XR + D_RNN
IN_WIDTH = SPLIT_XG + 2 * D_MODEL

kernel_name = "hybrid_swa_rglru_diffusion_step"


def rms_norm(x, w):
    xf = x.astype(jnp.float32)
    y = xf * lax.rsqrt(jnp.mean(xf * xf, axis=-1, keepdims=True) + EPS)
    return (y * w.astype(jnp.float32)).astype(x.dtype)


def modulate(x, norm_w, shift, scale):
    return rms_norm(x, norm_w) * (1 + scale) + shift


def adaln(cond, w, b):
    return jax.nn.silu(cond) @ w + b


def axial_rope(n_tokens):
    rows = n_tokens // GRID_W
    row = jnp.broadcast_to(jnp.arange(rows)[:, None], (rows, GRID_W)).reshape(-1).astype(jnp.float32)
    col = jnp.broadcast_to(jnp.arange(GRID_W)[None, :], (rows, GRID_W)).reshape(-1).astype(jnp.float32)
    freqs = ROPE_THETA ** (-jnp.arange(0, ROPE_AXIS_DIM, 2, dtype=jnp.float32) / ROPE_AXIS_DIM)
    ang_r = row[:, None] * freqs
    ang_c = col[:, None] * freqs
    return (jnp.cos(ang_r), jnp.sin(ang_r), jnp.cos(ang_c), jnp.sin(ang_c))


def rotate(x, cos, sin):
    x1, x2 = jnp.split(x, 2, axis=-1)
    cos = cos[:, None, :]
    sin = sin[:, None, :]
    return jnp.concatenate([x1 * cos - x2 * sin, x2 * cos + x1 * sin], axis=-1)


def apply_axial_rope(x, rope):
    cr, sr, cc, sc = rope
    x_row, x_col = jnp.split(x, 2, axis=-1)
    return jnp.concatenate([rotate(x_row, cr, sr), rotate(x_col, cc, sc)], axis=-1).astype(x.dtype)


def in_projection(h, p):
    z = h @ p["w_in"]
    q, k, v, xr, xg, gates = jnp.split(z, [SPLIT_Q, SPLIT_K, SPLIT_V, SPLIT_XR, SPLIT_XG], axis=-1)
    B, T = h.shape[:2]
    q = rms_norm(q.reshape(B, T, N_Q_HEADS, HEAD_DIM), p["q_norm_w"])
    k = rms_norm(k.reshape(B, T, N_KV_HEADS, HEAD_DIM), p["k_norm_w"])
    v = v.reshape(B, T, N_KV_HEADS, HEAD_DIM)
    return q, k, v, xr, xg, gates


def sink_softmax(s, sink):
    sk = sink.astype(jnp.float32)[None, :, :, None, None]
    m = jnp.maximum(jnp.max(s, axis=-1, keepdims=True), sk)
    e = jnp.exp(s - m)
    return e / (jnp.sum(e, axis=-1, keepdims=True) + jnp.exp(sk - m))


def context_attention(q, k, v, sink):
    B, S = q.shape[:2]
    nb = S // BLOCK
    qb = q.reshape(B, nb, BLOCK, N_KV_HEADS, Q_PER_KV, HEAD_DIM).swapaxes(0, 1)
    sk = sink.reshape(N_KV_HEADS, Q_PER_KV)

    def one(qblk):
        s = jnp.einsum("bqhgd,bkhd->bhgqk", qblk, k).astype(jnp.float32) * SCALE
        pr = sink_softmax(s, sk).astype(v.dtype)
        return jnp.einsum("bhgqk,bkhd->bqhgd", pr, v)

    o = lax.map(one, qb)
    return o.swapaxes(0, 1).reshape(B, S, ATTN_WIDTH)


def banded_attention(q, k, v, ctx_k, ctx_v, sink):
    B, T = q.shape[:2]
    nb = T // BLOCK
    P = ctx_k.shape[1]
    pad = ((0, 0), (BLOCK, BLOCK), (0, 0), (0, 0))
    kp = jnp.pad(k, pad).reshape(B, nb + 2, BLOCK, N_KV_HEADS, HEAD_DIM)
    vp = jnp.pad(v, pad).reshape(B, nb + 2, BLOCK, N_KV_HEADS, HEAD_DIM)
    kwin = jnp.concatenate([kp[:, :-2], kp[:, 1:-1], kp[:, 2:]], axis=2)
    vwin = jnp.concatenate([vp[:, :-2], vp[:, 1:-1], vp[:, 2:]], axis=2)
    qb = q.reshape(B, nb, BLOCK, N_KV_HEADS, Q_PER_KV, HEAD_DIM)
    qq = jnp.arange(BLOCK)[:, None]
    kk = jnp.arange(3 * BLOCK)[None, :]
    band = jnp.abs(kk - BLOCK - qq) <= WINDOW
    sk = sink.reshape(N_KV_HEADS, Q_PER_KV)

    def one(args):
        qblk, kblk, vblk, i = args
        kpos = i * BLOCK - BLOCK + kk
        mask = band & (kpos >= 0) & (kpos < T)
        s_loc = jnp.einsum("bqhgd,bkhd->bhgqk", qblk, kblk).astype(jnp.float32) * SCALE
        s_loc = jnp.where(mask, s_loc, NEG)
        s_ctx = jnp.einsum("bqhgd,bkhd->bhgqk", qblk, ctx_k).astype(jnp.float32) * SCALE
        pr = sink_softmax(jnp.concatenate([s_ctx, s_loc], axis=-1), sk).astype(v.dtype)
        return (jnp.einsum("bhgqk,bkhd->bqhgd", pr[..., :P], ctx_v)
                + jnp.einsum("bhgqk,bkhd->bqhgd", pr[..., P:], vblk))

    xs = (qb.swapaxes(0, 1), kwin.swapaxes(0, 1), vwin.swapaxes(0, 1), jnp.arange(nb))
    o = lax.map(one, xs)
    return o.swapaxes(0, 1).reshape(B, T, ATTN_WIDTH)


def centred_conv(x, w, b):
    T = x.shape[1]
    xp = jnp.pad(x, ((0, 0), (CONV_LEFT, CONV_W - 1 - CONV_LEFT), (0, 0)))
    y = w[0] * xp[:, 0:T]
    for tap in range(1, CONV_W):
        y = y + w[tap] * xp[:, tap:tap + T]
    return y + b


def block_diag(x, w, b):
    xb = x.reshape(*x.shape[:-1], N_RNN_BLOCKS, RNN_BLOCK_W)
    return jnp.einsum("btnc,ncd->btnd", xb, w).reshape(x.shape) + b


def lru_combine(e1, e2):
    a1, b1 = e1
    a2, b2 = e2
    return a1 * a2, a2 * b1 + b2


def rglru_direction(x, h0, lam, wa, ba, wi, bi, reverse):
    r = jax.nn.sigmoid(block_diag(x, wa, ba)).astype(jnp.float32)
    ig = jax.nn.sigmoid(block_diag(x, wi, bi)).astype(jnp.float32)
    log_a = -LRU_C * r * jax.nn.softplus(-lam.astype(jnp.float32))
    a = jnp.exp(log_a)
    bterm = jnp.sqrt(-jnp.expm1(2.0 * log_a)) * (ig * x.astype(jnp.float32))
    h0 = h0.astype(jnp.float32)
    if reverse:
        bterm = bterm.at[:, -1].add(a[:, -1] * h0)
    else:
        bterm = bterm.at[:, 0].add(a[:, 0] * h0)
    _, h = lax.associative_scan(lru_combine, (a, bterm), reverse=reverse, axis=1)
    final = h[:, 0] if reverse else h[:, -1]
    return h, final


def lru_branch(xr, xg, h0, p):
    xc = centred_conv(xr, p["conv_w"], p["conv_b"])
    hf, sf = rglru_direction(xc, h0[:, 0], p["lru_lambda"][0], p["lru_wa"][0], p["lru_ba"][0],
                             p["lru_wi"][0], p["lru_bi"][0], reverse=False)
    hb, sb = rglru_direction(xc, h0[:, 1], p["lru_lambda"][1], p["lru_wa"][1], p["lru_ba"][1],
                             p["lru_wi"][1], p["lru_bi"][1], reverse=True)
    y = (hf + hb).astype(xr.dtype) * jax.nn.gelu(xg)
    return y, jnp.stack([sf, sb], axis=1)


def merge_branches(attn, lru, gates, p):
    g_attn, g_lru = jnp.split(jax.nn.sigmoid(gates), 2, axis=-1)
    merged = g_attn * (attn @ p["w_attn_o"]) + g_lru * (lru @ p["w_lru_o"])
    return merged @ p["w_out"]


def context_mixer(h, p):
    q, k, v, xr, xg, gates = in_projection(h, p)
    attn = context_attention(q, k, v, p["attn_sink"])
    h0 = jnp.zeros((h.shape[0], 2, D_RNN), jnp.float32)
    lru, states = lru_branch(xr, xg, h0, p)
    return merge_branches(attn, lru, gates, p), k, v, states


def latent_mixer(h, p, ctx_k, ctx_v, ctx_state, rope):
    q, k, v, xr, xg, gates = in_projection(h, p)
    q = apply_axial_rope(q, rope)
    k = apply_axial_rope(k, rope)
    attn = banded_attention(q, k, v, ctx_k, ctx_v, p["attn_sink"])
    lru, _ = lru_branch(xr, xg, ctx_state, p)
    return merge_branches(attn, lru, gates, p)


def sq_relu_mlp(h, p):
    return jnp.square(jax.nn.relu(h @ p["mlp_w1"])) @ p["mlp_w2"]


def setup_inputs(seed: int = 0) -> dict:
    key = jax.random.key(seed)
    ks = jax.random.split(key, 32)
    f32 = jnp.float32

    def nrm(k, shape, scale):
        return jax.random.normal(k, shape, f32) * scale

    u = jax.random.uniform(ks[17], (DEPTH, 2, D_RNN), f32, 0.9, 0.999)
    return {
        "x_prompt": nrm(ks[0], (BATCH, SEQ, D_MODEL), 1.0),
        "x_sample": nrm(ks[1], (DEC_BATCH, DEC_SEQ, D_MODEL), 1.0),
        "cache_k": nrm(ks[2], (DEC_BATCH, DEPTH, PAST_LEN, N_KV_HEADS, HEAD_DIM), 1.0),
        "cache_v": nrm(ks[3], (DEC_BATCH, DEPTH, PAST_LEN, N_KV_HEADS, HEAD_DIM), 1.0),
        "state_lru": nrm(ks[4], (DEC_BATCH, DEPTH, 2, D_RNN), 0.5),
        "c": nrm(ks[5], (DEC_BATCH, D_MODEL), 1.0),
        "c_ctx": nrm(ks[6], (D_MODEL,), 1.0),
        "ada_w": nrm(ks[7], (DEPTH, D_MODEL, N_MOD * D_MODEL), 0.5 * D_MODEL ** -0.5),
        "ada_b": nrm(ks[8], (DEPTH, N_MOD * D_MODEL), 0.02),
        "norm1_w": 1.0 + nrm(ks[9], (DEPTH, D_MODEL), 0.05),
        "norm2_w": 1.0 + nrm(ks[10], (DEPTH, D_MODEL), 0.05),
        "w_in": nrm(ks[11], (DEPTH, D_MODEL, IN_WIDTH), D_MODEL ** -0.5),
        "q_norm_w": 1.0 + nrm(ks[12], (DEPTH, HEAD_DIM), 0.05),
        "k_norm_w": 1.0 + nrm(ks[13], (DEPTH, HEAD_DIM), 0.05),
        "attn_sink": nrm(ks[14], (DEPTH, N_Q_HEADS), 0.5),
        "conv_w": nrm(ks[15], (DEPTH, CONV_W, D_RNN), CONV_W ** -0.5),
        "conv_b": nrm(ks[16], (DEPTH, D_RNN), 0.02),
        "lru_lambda": jnp.log(u) - jnp.log1p(-u),
        "lru_wa": nrm(ks[18], (DEPTH, 2, N_RNN_BLOCKS, RNN_BLOCK_W, RNN_BLOCK_W), RNN_BLOCK_W ** -0.5),
        "lru_ba": nrm(ks[19], (DEPTH, 2, D_RNN), 0.02),
        "lru_wi": nrm(ks[20], (DEPTH, 2, N_RNN_BLOCKS, RNN_BLOCK_W, RNN_BLOCK_W), RNN_BLOCK_W ** -0.5),
        "lru_bi": nrm(ks[21], (DEPTH, 2, D_RNN), 0.02),
        "w_attn_o": nrm(ks[22], (DEPTH, ATTN_WIDTH, D_MODEL), ATTN_WIDTH ** -0.5),
        "w_lru_o": nrm(ks[23], (DEPTH, D_RNN, D_MODEL), D_RNN ** -0.5),
        "w_out": nrm(ks[24], (DEPTH, D_MODEL, D_MODEL), D_MODEL ** -0.5),
        "mlp_w1": nrm(ks[25], (DEPTH, D_MODEL, D_FF), D_MODEL ** -0.5),
        "mlp_w2": nrm(ks[26], (DEPTH, D_FF, D_MODEL), D_FF ** -0.5),
    }


def reference(x_prompt, x_sample, cache_k, cache_v, state_lru, c, c_ctx,
              ada_w, ada_b, norm1_w, norm2_w, w_in, q_norm_w, k_norm_w, attn_sink,
              conv_w, conv_b, lru_lambda, lru_wa, lru_ba, lru_wi, lru_bi,
              w_attn_o, w_lru_o, w_out, mlp_w1, mlp_w2):
    rope = axial_rope(x_sample.shape[1])
    xp = x_prompt
    xs = x_sample
    new_k, new_v, new_s = [], [], []
    for l in range(DEPTH):
        p = {
            "w_in": w_in[l], "q_norm_w": q_norm_w[l], "k_norm_w": k_norm_w[l],
            "attn_sink": attn_sink[l], "conv_w": conv_w[l], "conv_b": conv_b[l],
            "lru_lambda": lru_lambda[l], "lru_wa": lru_wa[l], "lru_ba": lru_ba[l],
            "lru_wi": lru_wi[l], "lru_bi": lru_bi[l], "w_attn_o": w_attn_o[l],
            "w_lru_o": w_lru_o[l], "w_out": w_out[l], "mlp_w1": mlp_w1[l], "mlp_w2": mlp_w2[l],
        }
        mod_c = adaln(c_ctx, ada_w[l], ada_b[l])[None, None, :]
        s1, sc1, g1, s2, sc2, g2 = jnp.split(mod_c, N_MOD, axis=-1)
        out, k_l, v_l, st_l = context_mixer(modulate(xp, norm1_w[l], s1, sc1), p)
        xp = xp + g1 * out
        xp = xp + g2 * sq_relu_mlp(modulate(xp, norm2_w[l], s2, sc2), p)
        new_k.append(k_l)
        new_v.append(v_l)
        new_s.append(st_l)
        mod_s = adaln(c, ada_w[l], ada_b[l])[:, None, :]
        s1, sc1, g1, s2, sc2, g2 = jnp.split(mod_s, N_MOD, axis=-1)
        out = latent_mixer(modulate(xs, norm1_w[l], s1, sc1), p,
                           cache_k[:, l], cache_v[:, l], state_lru[:, l], rope)
        xs = xs + g1 * out
        xs = xs + g2 * sq_relu_mlp(modulate(xs, norm2_w[l], s2, sc2), p)
    new_k_arr = jnp.stack(new_k, axis=1)
    new_v_arr = jnp.stack(new_v, axis=1)
    new_s_arr = jnp.stack(new_s, axis=1)
    return (xp, xs, new_k_arr, new_v_arr, new_s_arr)
```

```python
import functools

import jax
import jax.numpy as jnp
from jax import lax
from jax.experimental import pallas as pl
from jax.experimental.pallas import tpu as pltpu

D_MODEL = 1024
DEPTH = 4
GRID_W = 64
HEAD_DIM = 128
N_Q_HEADS = 8
N_KV_HEADS = 2
Q_PER_KV = N_Q_HEADS // N_KV_HEADS
ATTN_WIDTH = N_Q_HEADS * HEAD_DIM
KV_WIDTH = N_KV_HEADS * HEAD_DIM
WINDOW = 128
BLOCK = 128
SCALE = HEAD_DIM ** -0.5
ROPE_AXIS_DIM = HEAD_DIM // 2
ROPE_THETA = 10000.0
D_RNN = D_MODEL
N_RNN_BLOCKS = 8
RNN_BLOCK_W = D_RNN // N_RNN_BLOCKS
CONV_W = 4
CONV_LEFT = 2
LRU_C = 8.0
D_FF = 4 * D_MODEL
N_MOD = 6
EPS = 1e-6
NEG = -1e30
SPLIT_Q = ATTN_WIDTH
SPLIT_K = SPLIT_Q + KV_WIDTH
SPLIT_V = SPLIT_K + KV_WIDTH
SPLIT_XR = SPLIT_V + D_RNN
SPLIT_XG = SPLIT_XR + D_RNN
IN_WIDTH = SPLIT_XG + 2 * D_MODEL

LANES = 128
SUBLANES = 8
VMEM_LIMIT = 56 * 1024 * 1024

ROW_TILE = 256
N_COND = 8
SEG_PAD = 4
LRU_CHUNK = 256

F32 = jnp.float32
BF16 = jnp.bfloat16


def _sigmoid(x):
    return 0.5 * jnp.tanh(0.5 * x) + 0.5


def _gelu_tanh(x):
    c = 0.7978845608028654
    return 0.5 * x * (1.0 + jnp.tanh(c * (x + 0.044715 * (x * x * x))))


def _softplus(x):
    return jnp.maximum(x, 0.0) + jnp.log1p(jnp.exp(-jnp.abs(x)))


def _dot(a, b):
    return jnp.dot(a, b, preferred_element_type=F32)


def _dot_nt(a, b):
    return lax.dot_general(a, b, (((1,), (1,)), ((), ())), preferred_element_type=F32)


def _rms(x):
    return x * lax.rsqrt(jnp.mean(x * x, axis=-1, keepdims=True) + EPS)


def _resident(shape, index_map):
    return pl.BlockSpec(shape, index_map, pipeline_mode=pl.Buffered(1))


def _params():
    return pltpu.CompilerParams(vmem_limit_bytes=VMEM_LIMIT)


def _adaln_kernel(cond_ref, w_ref, b_ref, o_ref):
    cnd = cond_ref[...]
    act = (cnd * _sigmoid(cnd)).astype(BF16)
    o_ref[...] = _dot(act, w_ref[...].astype(BF16)) + b_ref[...]


def _adaln(cond, ada_w, ada_b):
    tn = 1536
    width = N_MOD * D_MODEL
    return pl.pallas_call(
        _adaln_kernel,
        grid=(DEPTH, width // tn),
        in_specs=[
            pl.BlockSpec((N_COND, D_MODEL), lambda l, j: (0, 0)),
            pl.BlockSpec((None, D_MODEL, tn), lambda l, j: (l, 0, j)),
            pl.BlockSpec((None, 1, tn), lambda l, j: (l, 0, j)),
        ],
        out_specs=pl.BlockSpec((None, N_COND, tn), lambda l, j: (l, 0, j)),
        out_shape=jax.ShapeDtypeStruct((DEPTH, N_COND, width), F32),
        compiler_params=_params(),
        name="adaln",
    )(cond, ada_w, ada_b.reshape(DEPTH, 1, width))


def _rope(x, cs, sn):
    lane = lax.broadcasted_iota(jnp.int32, x.shape, 1)
    half = ROPE_AXIS_DIM // 2
    swapped = jnp.where((lane & (ROPE_AXIS_DIM - 1)) < half,
                        pltpu.roll(x, HEAD_DIM - half, 1), pltpu.roll(x, half, 1))
    return x * cs + swapped * sn


def _inproj_kernel(*refs, rope, keep_kv):
    x_ref, mod_ref, nw_ref, w_ref, qw_ref, kw_ref = refs[:6]
    refs = refs[6:]
    if rope:
        cs_ref, sn_ref = refs[:2]
        refs = refs[2:]
    q_ref, k_ref, v_ref, xr_ref, xg_ref, g_ref = refs[:6]
    if keep_kv:
        k32_ref, v32_ref = refs[6:8]

    y = _rms(x_ref[...]) * nw_ref[...]
    h = (y * (1.0 + mod_ref[1:2, :]) + mod_ref[0:1, :]).astype(BF16)

    def head(z, w):
        o = _rms(z) * w
        return _rope(o, cs_ref[...], sn_ref[...]) if rope else o

    zq = _dot(h, w_ref[:, 0:SPLIT_Q])
    for hd in range(N_Q_HEADS):
        sl = slice(hd * HEAD_DIM, (hd + 1) * HEAD_DIM)
        q_ref[:, sl] = head(zq[:, sl], qw_ref[...]).astype(BF16)

    zkv = _dot(h, w_ref[:, SPLIT_Q:SPLIT_V])
    for hd in range(N_KV_HEADS):
        sl = slice(hd * HEAD_DIM, (hd + 1) * HEAD_DIM)
        kn = head(zkv[:, sl], kw_ref[...])
        k_ref[:, sl] = kn.astype(BF16)
        if keep_kv:
            k32_ref[:, sl] = kn
    zv = zkv[:, KV_WIDTH:2 * KV_WIDTH]
    v_ref[...] = zv.astype(BF16)
    if keep_kv:
        v32_ref[...] = zv

    xr_ref[...] = _dot(h, w_ref[:, SPLIT_V:SPLIT_XR])
    xg_ref[...] = _dot(h, w_ref[:, SPLIT_XR:SPLIT_XG])
    g_ref[:, 0:D_MODEL] = _dot(h, w_ref[:, SPLIT_XG:SPLIT_XG + D_MODEL])
    g_ref[:, D_MODEL:] = _dot(h, w_ref[:, SPLIT_XG + D_MODEL:])


def _inproj(x, mod, l, cond_of_tile, norm1_w, w_in, q_norm_w, k_norm_w, rope_tabs, keep_kv):
    n = x.shape[0]
    tm = ROW_TILE
    rope = rope_tabs is not None
    row = lambda i: (i, 0)
    in_specs = [
        pl.BlockSpec((tm, D_MODEL), row),
        pl.BlockSpec((None, None, N_MOD, D_MODEL), lambda i: (l, cond_of_tile(i), 0, 0)),
        _resident((None, 1, D_MODEL), lambda i: (l, 0, 0)),
        _resident((None, D_MODEL, IN_WIDTH), lambda i: (l, 0, 0)),
        _resident((None, 1, HEAD_DIM), lambda i: (l, 0, 0)),
        _resident((None, 1, HEAD_DIM), lambda i: (l, 0, 0)),
    ]
    args = [x, mod, norm1_w, w_in, q_norm_w, k_norm_w]
    if rope:
        tiles_per_seq = rope_tabs[0].shape[0] // tm
        tab = pl.BlockSpec((tm, HEAD_DIM), lambda i: (i % tiles_per_seq, 0))
        in_specs += [tab, tab]
        args += list(rope_tabs)
    out_shape = [
        jax.ShapeDtypeStruct((n, ATTN_WIDTH), BF16),
        jax.ShapeDtypeStruct((n, KV_WIDTH), BF16),
        jax.ShapeDtypeStruct((n, KV_WIDTH), BF16),
        jax.ShapeDtypeStruct((n, D_RNN), F32),
        jax.ShapeDtypeStruct((n, D_RNN), F32),
        jax.ShapeDtypeStruct((n, 2 * D_MODEL), F32),
    ]
    out_specs = [
        pl.BlockSpec((tm, ATTN_WIDTH), row),
        pl.BlockSpec((tm, KV_WIDTH), row),
        pl.BlockSpec((tm, KV_WIDTH), row),
        pl.BlockSpec((tm, D_RNN), row),
        pl.BlockSpec((tm, D_RNN), row),
        pl.BlockSpec((tm, 2 * D_MODEL), row),
    ]
    if keep_kv:
        out_shape += [jax.ShapeDtypeStruct((n, KV_WIDTH), F32)] * 2
        out_specs += [pl.BlockSpec((tm, KV_WIDTH), row)] * 2
    return pl.pallas_call(
        functools.partial(_inproj_kernel, rope=rope, keep_kv=keep_kv),
        grid=(n // tm,),
        in_specs=in_specs,
        out_specs=out_specs,
        out_shape=out_shape,
        compiler_params=_params(),
        name="inproj_rope" if rope else "inproj",
    )(*args)


def _stack_heads(q_at, g):
    return jnp.concatenate(
        [q_at(slice((g * Q_PER_KV + j) * HEAD_DIM, (g * Q_PER_KV + j + 1) * HEAD_DIM))
         for j in range(Q_PER_KV)], axis=0)


def _sink_column(sink_ref, g, rows):
    return jnp.concatenate(
        [jnp.full((rows, 1), sink_ref[g * Q_PER_KV + j], F32) for j in range(Q_PER_KV)], axis=0)


def _ctx_attn_kernel(sink_ref, q_ref, k_ref, v_ref, o_ref):
    bt, t = q_ref.shape[0], q_ref.shape[1]

    def body(b, carry):
        for g in range(N_KV_HEADS):
            sl = slice(g * HEAD_DIM, (g + 1) * HEAD_DIM)
            q4 = _stack_heads(lambda s: q_ref[b, :, s], g)
            s = _dot_nt(q4, k_ref[b, :, sl]) * SCALE
            sk = _sink_column(sink_ref, g, t)
            m = jnp.maximum(jnp.max(s, axis=-1, keepdims=True), sk)
            e = jnp.exp(s - m)
            den = jnp.sum(e, axis=-1, keepdims=True) + jnp.exp(sk - m)
            o = _dot(e.astype(BF16), v_ref[b, :, sl]) / den
            for j in range(Q_PER_KV):
                hs = slice((g * Q_PER_KV + j) * HEAD_DIM, (g * Q_PER_KV + j + 1) * HEAD_DIM)
                o_ref[b, :, hs] = o[j * t:(j + 1) * t].astype(BF16)
        return carry

    lax.fori_loop(0, bt, body, 0)


def _ctx_attn(sink, q, k, v):
    b, t = q.shape[0], q.shape[1]
    bt = 4
    blk = lambda w: pl.BlockSpec((bt, t, w), lambda i: (i, 0, 0))
    return pl.pallas_call(
        _ctx_attn_kernel,
        grid=(b // bt,),
        in_specs=[pl.BlockSpec(memory_space=pltpu.SMEM), blk(ATTN_WIDTH), blk(KV_WIDTH), blk(KV_WIDTH)],
        out_specs=blk(ATTN_WIDTH),
        out_shape=jax.ShapeDtypeStruct((b, t, ATTN_WIDTH), BF16),
        compiler_params=_params(),
        name="ctx_attn",
    )(sink, q, k, v)


def _band_attn_kernel(sink_ref, q_ref, kp_ref, kc_ref, kn_ref, vp_ref, vc_ref, vn_ref,
                      ck_ref, cv_ref, o_ref, *, seq_len):
    i = pl.program_id(1)
    rows = Q_PER_KV * BLOCK
    qq = lax.broadcasted_iota(jnp.int32, (rows, 3 * BLOCK), 0) & (BLOCK - 1)
    kk = lax.broadcasted_iota(jnp.int32, (rows, 3 * BLOCK), 1)
    kpos = i * BLOCK - BLOCK + kk
    mask = (jnp.abs(kk - BLOCK - qq) <= WINDOW) & (kpos >= 0) & (kpos < seq_len)
    for g in range(N_KV_HEADS):
        sl = slice(g * HEAD_DIM, (g + 1) * HEAD_DIM)
        q4 = _stack_heads(lambda s: q_ref[:, s], g)
        kloc = jnp.concatenate([kp_ref[:, sl], kc_ref[:, sl], kn_ref[:, sl]], axis=0)
        vloc = jnp.concatenate([vp_ref[:, sl], vc_ref[:, sl], vn_ref[:, sl]], axis=0)
        s_loc = jnp.where(mask, _dot_nt(q4, kloc) * SCALE, NEG)
        s_ctx = _dot_nt(q4, ck_ref[:, sl]) * SCALE
        sk = _sink_column(sink_ref, g, BLOCK)
        m = jnp.maximum(jnp.maximum(jnp.max(s_ctx, axis=-1, keepdims=True),
                                    jnp.max(s_loc, axis=-1, keepdims=True)), sk)
        e_ctx = jnp.exp(s_ctx - m)
        e_loc = jnp.exp(s_loc - m)
        den = (jnp.sum(e_ctx, axis=-1, keepdims=True) + jnp.sum(e_loc, axis=-1, keepdims=True)
               + jnp.exp(sk - m))
        o = (_dot(e_ctx.astype(BF16), cv_ref[:, sl]) + _dot(e_loc.astype(BF16), vloc)) / den
        for j in range(Q_PER_KV):
            hs = slice((g * Q_PER_KV + j) * HEAD_DIM, (g * Q_PER_KV + j + 1) * HEAD_DIM)
            o_ref[:, hs] = o[j * BLOCK:(j + 1) * BLOCK].astype(BF16)


def _band_attn(sink, q, k, v, ctx_k, ctx_v, l):
    b, t = q.shape[0], q.shape[1]
    nb = t // BLOCK
    past = ctx_k.shape[2]
    cur = lambda w: pl.BlockSpec((None, BLOCK, w), lambda bi, i: (bi, i, 0))
    prev = pl.BlockSpec((None, BLOCK, KV_WIDTH), lambda bi, i: (bi, jnp.maximum(i - 1, 0), 0))
    nxt = pl.BlockSpec((None, BLOCK, KV_WIDTH), lambda bi, i: (bi, jnp.minimum(i + 1, nb - 1), 0))
    ctx = pl.BlockSpec((None, None, past, KV_WIDTH), lambda bi, i: (bi, l, 0, 0))
    return pl.pallas_call(
        functools.partial(_band_attn_kernel, seq_len=t),
        grid=(b, nb),
        in_specs=[pl.BlockSpec(memory_space=pltpu.SMEM), cur(ATTN_WIDTH),
                  prev, cur(KV_WIDTH), nxt, prev, cur(KV_WIDTH), nxt, ctx, ctx],
        out_specs=cur(ATTN_WIDTH),
        out_shape=jax.ShapeDtypeStruct((b, t, ATTN_WIDTH), BF16),
        compiler_params=_params(),
        name="band_attn",
    )(sink, q, k, k, k, v, v, v, ctx_k, ctx_v)


def _lru_kernel(xr_ref, xg_ref, cw_ref, cb_ref, lam_ref, wg_ref, bg_ref, h0_ref,
                y_ref, st_ref, xpad, xconv, af, bf, ab, bb, *, seq_len, seg):
    t = seq_len
    cb = xr_ref.shape[1]
    n_slab = cb // LANES
    rc = min(LRU_CHUNK, t)
    rows = SUBLANES * seg

    xpad[0:SUBLANES, :] = jnp.zeros((SUBLANES, cb), F32)
    xpad[SUBLANES + t:2 * SUBLANES + t, :] = jnp.zeros((SUBLANES, cb), F32)
    xpad[SUBLANES:SUBLANES + t, :] = xr_ref[...]

    base = SUBLANES - CONV_LEFT
    xc = cw_ref[0:1, :] * xpad[base:base + t, :]
    for tap in range(1, CONV_W):
        xc = xc + cw_ref[tap:tap + 1, :] * xpad[base + tap:base + tap + t, :]
    xconv[...] = xc + cb_ref[...]

    for s in range(n_slab):
        af[s, t:rows, :] = jnp.ones((rows - t, LANES), F32)
        ab[s, t:rows, :] = jnp.ones((rows - t, LANES), F32)
        bf[s, t:rows, :] = jnp.zeros((rows - t, LANES), F32)
        bb[s, t:rows, :] = jnp.zeros((rows - t, LANES), F32)

    log_decay = -LRU_C * _softplus(-lam_ref[...])

    def gates_chunk(ci, carry):
        r0 = pl.multiple_of(ci * rc, rc)
        xc = xconv[pl.ds(r0, rc), :]
        xcb = xc.astype(BF16)
        for s in range(n_slab):
            sl = slice(s * LANES, (s + 1) * LANES)
            g = _dot(xcb[:, sl], wg_ref[s]) + bg_ref[s]
            xs = xc[:, sl]
            for d, (a_scr, b_scr) in enumerate(((af, bf), (ab, bb))):
                off = 2 * d * LANES
                r = _sigmoid(g[:, off:off + LANES])
                ig = _sigmoid(g[:, off + LANES:off + 2 * LANES])
                log_a = log_decay[d:d + 1, sl] * r
                a = jnp.exp(log_a)
                a_scr[s, pl.ds(r0, rc), :] = a
                b_scr[s, pl.ds(r0, rc), :] = jnp.sqrt(-jnp.tanh(log_a) * (a * a + 1.0)) * (ig * xs)
        return carry

    lax.fori_loop(0, t // rc, gates_chunk, 0)

    def seg_rows(i):
        return pl.ds(i, SUBLANES, stride=seg)

    ones = jnp.ones((SUBLANES, LANES), F32)
    zeros = jnp.zeros((SUBLANES, LANES), F32)

    def totals_step(i, carry):
        out = []
        ib = seg - 1 - i
        for s in range(n_slab):
            pf, hf, pb, hb = carry[s]
            a = af[s, seg_rows(i), :]
            a2 = ab[s, seg_rows(ib), :]
            out.append((pf * a, a * hf + bf[s, seg_rows(i), :],
                        pb * a2, a2 * hb + bb[s, seg_rows(ib), :]))
        return tuple(out)

    totals = lax.fori_loop(0, seg, totals_step, tuple((ones, zeros, ones, zeros) for _ in range(n_slab)))

    starts = []
    for s in range(n_slab):
        sl = slice(s * LANES, (s + 1) * LANES)
        pf, hf, pb, hb = totals[s]
        c = h0_ref[0:1, sl]
        fwd = []
        for j in range(SUBLANES):
            fwd.append(c)
            c = hf[j:j + 1] + pf[j:j + 1] * c
        st_ref[0:1, sl] = c
        c = h0_ref[1:2, sl]
        bwd = [None] * SUBLANES
        for j in reversed(range(SUBLANES)):
            bwd[j] = c
            c = hb[j:j + 1] + pb[j:j + 1] * c
        st_ref[1:2, sl] = c
        starts.append((jnp.concatenate(fwd, axis=0), jnp.concatenate(bwd, axis=0)))

    def states_step(i, carry):
        out = []
        ib = seg - 1 - i
        for s in range(n_slab):
            hf, hb = carry[s]
            hf = af[s, seg_rows(i), :] * hf + bf[s, seg_rows(i), :]
            hb = ab[s, seg_rows(ib), :] * hb + bb[s, seg_rows(ib), :]
            bf[s, seg_rows(i), :] = hf
            bb[s, seg_rows(ib), :] = hb
            out.append((hf, hb))
        return tuple(out)

    lax.fori_loop(0, seg, states_step, tuple(starts))

    def out_chunk(ci, carry):
        r0 = pl.multiple_of(ci * rc, rc)
        for s in range(n_slab):
            sl = slice(s * LANES, (s + 1) * LANES)
            hsum = bf[s, pl.ds(r0, rc), :] + bb[s, pl.ds(r0, rc), :]
            y_ref[pl.ds(r0, rc), sl] = (hsum * _gelu_tanh(xg_ref[pl.ds(r0, rc), sl])).astype(BF16)
        return carry

    lax.fori_loop(0, t // rc, out_chunk, 0)


def _lru(xr, xg, h0, l, seq_len, conv_w, conv_b, lam, wg, bg, cb):
    n = xr.shape[0]
    b = n // seq_len
    n_slab = cb // LANES
    seg = seq_len // SUBLANES + SEG_PAD
    rows = SUBLANES * seg
    col = lambda bi, j: (bi, j)
    par = lambda r: pl.BlockSpec((None, r, cb), lambda bi, j: (l, 0, j))
    scan_buf = pltpu.VMEM((n_slab, rows, LANES), F32)
    return pl.pallas_call(
        functools.partial(_lru_kernel, seq_len=seq_len, seg=seg),
        grid=(b, D_RNN // cb),
        in_specs=[
            pl.BlockSpec((seq_len, cb), col),
            pl.BlockSpec((seq_len, cb), col),
            par(CONV_W), par(1), par(2),
            pl.BlockSpec((None, n_slab, RNN_BLOCK_W, 4 * RNN_BLOCK_W), lambda bi, j: (l, j, 0, 0)),
            pl.BlockSpec((None, n_slab, 1, 4 * RNN_BLOCK_W), lambda bi, j: (l, j, 0, 0)),
            pl.BlockSpec((None, 2, cb), lambda bi, j: (bi, 0, j)),
        ],
        out_specs=[
            pl.BlockSpec((seq_len, cb), col),
            pl.BlockSpec((None, 2, cb), lambda bi, j: (bi, 0, j)),
        ],
        out_shape=[
            jax.ShapeDtypeStruct((n, D_RNN), BF16),
            jax.ShapeDtypeStruct((b, 2, D_RNN), F32),
        ],
        scratch_shapes=[pltpu.VMEM((seq_len + 2 * SUBLANES, cb), F32),
                        pltpu.VMEM((seq_len, cb), F32),
                        scan_buf, scan_buf, scan_buf, scan_buf],
        compiler_params=_params(),
        name="rglru",
    )(xr, xg, conv_w, conv_b, lam, wg, bg, h0)


def _post_kernel(x_ref, attn_ref, lru_ref, g_ref, mod_ref, nw_ref, wa_ref, wl_ref, wo_ref,
                 w1_ref, w2_ref, o_ref):
    a = _dot(attn_ref[...], wa_ref[...])
    b = _dot(lru_ref[...], wl_ref[...])
    merged = _sigmoid(g_ref[:, 0:D_MODEL]) * a + _sigmoid(g_ref[:, D_MODEL:]) * b
    x = x_ref[...] + mod_ref[2:3, :] * _dot(merged.astype(BF16), wo_ref[...])
    h = (_rms(x) * nw_ref[...] * (1.0 + mod_ref[4:5, :]) + mod_ref[3:4, :]).astype(BF16)
    ff = D_MODEL
    acc = None
    for c in range(D_FF // ff):
        hid = jnp.maximum(_dot(h, w1_ref[:, c * ff:(c + 1) * ff]), 0.0)
        part = _dot((hid * hid).astype(BF16), w2_ref[c * ff:(c + 1) * ff, :])
        acc = part if acc is None else acc + part
    o_ref[...] = x + mod_ref[5:6, :] * acc


def _post(x, attn, lru, gates, mod, l, cond_of_tile, norm2_w, w_attn_o, w_lru_o, w_out, w1, w2):
    n = x.shape[0]
    tm = ROW_TILE
    row = lambda i: (i, 0)
    sq = _resident((None, D_MODEL, D_MODEL), lambda i: (l, 0, 0))
    return pl.pallas_call(
        _post_kernel,
        grid=(n // tm,),
        in_specs=[
            pl.BlockSpec((tm, D_MODEL), row),
            pl.BlockSpec((tm, ATTN_WIDTH), row),
            pl.BlockSpec((tm, D_RNN), row),
            pl.BlockSpec((tm, 2 * D_MODEL), row),
            pl.BlockSpec((None, None, N_MOD, D_MODEL), lambda i: (l, cond_of_tile(i), 0, 0)),
            _resident((None, 1, D_MODEL), lambda i: (l, 0, 0)),
            sq, sq, sq,
            _resident((None, D_MODEL, D_FF), lambda i: (l, 0, 0)),
            _resident((None, D_FF, D_MODEL), lambda i: (l, 0, 0)),
        ],
        out_specs=pl.BlockSpec((tm, D_MODEL), row),
        out_shape=jax.ShapeDtypeStruct((n, D_MODEL), F32),
        input_output_aliases={0: 0},
        compiler_params=_params(),
        name="post",
    )(x, attn, lru, gates, mod, norm2_w, w_attn_o, w_lru_o, w_out, w1, w2)


def _rope_tables(n_tokens):
    rows = n_tokens // GRID_W
    row = jnp.broadcast_to(jnp.arange(rows)[:, None], (rows, GRID_W)).reshape(-1).astype(F32)
    col = jnp.broadcast_to(jnp.arange(GRID_W)[None, :], (rows, GRID_W)).reshape(-1).astype(F32)
    freqs = ROPE_THETA ** (-jnp.arange(0, ROPE_AXIS_DIM, 2, dtype=F32) / ROPE_AXIS_DIM)
    ang_r = row[:, None] * freqs
    ang_c = col[:, None] * freqs
    cr, sr, cc, sc = jnp.cos(ang_r), jnp.sin(ang_r), jnp.cos(ang_c), jnp.sin(ang_c)
    return (jnp.concatenate([cr, cr, cc, cc], axis=-1),
            jnp.concatenate([-sr, sr, -sc, sc], axis=-1))


def kernel(x_prompt, x_sample, cache_k, cache_v, state_lru, c, c_ctx, ada_w, ada_b, norm1_w, norm2_w, w_in, q_norm_w, k_norm_w, attn_sink, conv_w, conv_b, lru_lambda, lru_wa, lru_ba, lru_wi, lru_bi, w_attn_o, w_lru_o, w_out, mlp_w1, mlp_w2):
    bp, tp, _ = x_prompt.shape
    bs, ts, _ = x_sample.shape
    past = cache_k.shape[2]
    assert bs + 1 <= N_COND and tp == ROW_TILE and ts % ROW_TILE == 0

    cond = jnp.zeros((N_COND, D_MODEL), F32).at[0].set(c_ctx).at[1:1 + bs].set(c)
    mod = _adaln(cond, ada_w, ada_b).reshape(DEPTH, N_COND, N_MOD, D_MODEL)
    tiles_per_sample = ts // ROW_TILE
    cond_prompt = lambda i: 0
    cond_sample = lambda i: 1 + i // tiles_per_sample

    w_in_b = w_in.astype(BF16)
    w_attn_o_b = w_attn_o.astype(BF16)
    w_lru_o_b = w_lru_o.astype(BF16)
    w_out_b = w_out.astype(BF16)
    w1_b = mlp_w1.astype(BF16)
    w2_b = mlp_w2.astype(BF16)
    wg = jnp.concatenate([lru_wa[:, 0], lru_wi[:, 0], lru_wa[:, 1], lru_wi[:, 1]], axis=-1).astype(BF16)
    blk = lambda v: v.reshape(DEPTH, N_RNN_BLOCKS, 1, RNN_BLOCK_W)
    bg = jnp.concatenate([blk(lru_ba[:, 0]), blk(lru_bi[:, 0]), blk(lru_ba[:, 1]), blk(lru_bi[:, 1])], axis=-1)
    ctx_k = cache_k.reshape(bs, DEPTH, past, KV_WIDTH).astype(BF16)
    ctx_v = cache_v.reshape(bs, DEPTH, past, KV_WIDTH).astype(BF16)
    rope_tabs = _rope_tables(ts)

    n1 = norm1_w.reshape(DEPTH, 1, D_MODEL)
    n2 = norm2_w.reshape(DEPTH, 1, D_MODEL)
    qw = q_norm_w.reshape(DEPTH, 1, HEAD_DIM)
    kw = k_norm_w.reshape(DEPTH, 1, HEAD_DIM)
    cvb = conv_b.reshape(DEPTH, 1, D_RNN)
    zero_state = jnp.zeros((bp, 2, D_RNN), F32)

    xp = x_prompt.reshape(bp * tp, D_MODEL)
    xs = x_sample.reshape(bs * ts, D_MODEL)
    new_k, new_v, new_s = [], [], []
    for l in range(DEPTH):
        q, k, v, xr, xg, gates, k32, v32 = _inproj(xp, mod, l, cond_prompt, n1, w_in_b, qw, kw, None, True)
        attn = _ctx_attn(attn_sink[l], q.reshape(bp, tp, -1), k.reshape(bp, tp, -1), v.reshape(bp, tp, -1))
        lru, st = _lru(xr, xg, zero_state, l, tp, conv_w, cvb, lru_lambda, wg, bg, 256)
        xp = _post(xp, attn.reshape(bp * tp, -1), lru, gates, mod, l, cond_prompt, n2,
                   w_attn_o_b, w_lru_o_b, w_out_b, w1_b, w2_b)
        new_k.append(k32)
        new_v.append(v32)
        new_s.append(st)
        q, k, v, xr, xg, gates = _inproj(xs, mod, l, cond_sample, n1, w_in_b, qw, kw, rope_tabs, False)
        attn = _band_attn(attn_sink[l], q.reshape(bs, ts, -1), k.reshape(bs, ts, -1),
                          v.reshape(bs, ts, -1), ctx_k, ctx_v, l)
        lru, _ = _lru(xr, xg, state_lru[:, l], l, ts, conv_w, cvb, lru_lambda, wg, bg, 256)
        xs = _post(xs, attn.reshape(bs * ts, -1), lru, gates, mod, l, cond_sample, n2,
                   w_attn_o_b, w_lru_o_b, w_out_b, w1_b, w2_b)

    kv_shape = (bp, DEPTH, tp, N_KV_HEADS, HEAD_DIM)
    new_k_arr = jnp.stack([a.reshape(bp, tp, KV_WIDTH) for a in new_k], axis=1).reshape(kv_shape)
    new_v_arr = jnp.stack([a.reshape(bp, tp, KV_WIDTH) for a in new_v], axis=1).reshape(kv_shape)
    return (xp.reshape(bp, tp, D_MODEL), xs.reshape(bs, ts, D_MODEL), new_k_arr, new_v_arr,
            jnp.stack(new_s, axis=1))
```

```python
import functools

import jax
import jax.numpy as jnp
from jax import lax
from jax.experimental import pallas as pl
from jax.experimental.pallas import tpu as pltpu

D_MODEL = 1024
DEPTH = 4
GRID_W = 64
HEAD_DIM = 128
N_Q_HEADS = 8
N_KV_HEADS = 2
Q_PER_KV = N_Q_HEADS // N_KV_HEADS
ATTN_WIDTH = N_Q_HEADS * HEAD_DIM
KV_WIDTH = N_KV_HEADS * HEAD_DIM
WINDOW = 128
BLOCK = 128
SCALE = HEAD_DIM ** -0.5
ROPE_AXIS_DIM = HEAD_DIM // 2
ROPE_THETA = 10000.0
D_RNN = D_MODEL
N_RNN_BLOCKS = 8
RNN_BLOCK_W = D_RNN // N_RNN_BLOCKS
CONV_W = 4
CONV_LEFT = 2
LRU_C = 8.0
D_FF = 4 * D_MODEL
N_MOD = 6
EPS = 1e-6
NEG = -1e30
SPLIT_Q = ATTN_WIDTH
SPLIT_K = SPLIT_Q + KV_WIDTH
SPLIT_V = SPLIT_K + KV_WIDTH
SPLIT_XR = SPLIT_V + D_RNN
SPLIT_XG = SPLIT_XR + D_RNN
IN_WIDTH = SPLIT_XG + 2 * D_MODEL

LANES = 128
SUBLANES = 8
VMEM_LIMIT = 56 * 1024 * 1024

ROW_TILE = 256
N_COND = 8
SEG_PAD = 4
ATTN_ROWS = 64
LOG2E = 1.4426950408889634
LRU_CHUNK = 256
SCAN_CHAINS = 4
GATE_SLABS = 2

F32 = jnp.float32
BF16 = jnp.bfloat16


def _sigmoid(x):
    return 0.5 * jnp.tanh(0.5 * x) + 0.5


def _gelu_tanh(x):
    c = 0.7978845608028654
    return 0.5 * x * (1.0 + jnp.tanh(c * (x + 0.044715 * (x * x * x))))


def _softplus(x):
    return jnp.maximum(x, 0.0) + jnp.log1p(jnp.exp(-jnp.abs(x)))


def _dot(a, b):
    return jnp.dot(a, b, preferred_element_type=F32)


def _dot_nt(a, b):
    return lax.dot_general(a, b, (((1,), (1,)), ((), ())), preferred_element_type=F32)


def _rms(x):
    return x * lax.rsqrt(jnp.mean(x * x, axis=-1, keepdims=True) + EPS)


def _resident(shape, index_map):
    return pl.BlockSpec(shape, index_map, pipeline_mode=pl.Buffered(1))


def _params():
    return pltpu.CompilerParams(vmem_limit_bytes=VMEM_LIMIT)


def _adaln_kernel(cond_ref, w_ref, b_ref, o_ref):
    cnd = cond_ref[...]
    act = (cnd * _sigmoid(cnd)).astype(BF16)
    o_ref[...] = _dot(act, w_ref[...].astype(BF16)) + b_ref[...]


def _adaln(cond, ada_w, ada_b):
    tn = 1536
    width = N_MOD * D_MODEL
    return pl.pallas_call(
        _adaln_kernel,
        grid=(DEPTH, width // tn),
        in_specs=[
            pl.BlockSpec((N_COND, D_MODEL), lambda l, j: (0, 0)),
            pl.BlockSpec((None, D_MODEL, tn), lambda l, j: (l, 0, j)),
            pl.BlockSpec((None, 1, tn), lambda l, j: (l, 0, j)),
        ],
        out_specs=pl.BlockSpec((None, N_COND, tn), lambda l, j: (l, 0, j)),
        out_shape=jax.ShapeDtypeStruct((DEPTH, N_COND, width), F32),
        compiler_params=_params(),
        name="adaln",
    )(cond, ada_w, ada_b.reshape(DEPTH, 1, width))


def _rope(x, cs, sn):
    lane = lax.broadcasted_iota(jnp.int32, x.shape, 1)
    half = ROPE_AXIS_DIM // 2
    swapped = jnp.where((lane & (ROPE_AXIS_DIM - 1)) < half,
                        pltpu.roll(x, HEAD_DIM - half, 1), pltpu.roll(x, half, 1))
    return x * cs + swapped * sn


def _inproj_kernel(*refs, rope, keep_kv):
    x_ref, mod_ref, nw_ref, w_ref, qw_ref, kw_ref = refs[:6]
    refs = refs[6:]
    if rope:
        cs_ref, sn_ref = refs[:2]
        refs = refs[2:]
    q_ref, k_ref, v_ref, xr_ref, xg_ref, g_ref = refs[:6]
    if keep_kv:
        k32_ref, v32_ref = refs[6:8]

    y = _rms(x_ref[...]) * nw_ref[...]
    h = (y * (1.0 + mod_ref[1:2, :]) + mod_ref[0:1, :]).astype(BF16)

    def head(z, w):
        o = _rms(z) * w
        return _rope(o, cs_ref[...], sn_ref[...]) if rope else o

    zq = _dot(h, w_ref[:, 0:SPLIT_Q])
    for hd in range(N_Q_HEADS):
        sl = slice(hd * HEAD_DIM, (hd + 1) * HEAD_DIM)
        q_ref[:, sl] = head(zq[:, sl], qw_ref[...]).astype(BF16)

    zkv = _dot(h, w_ref[:, SPLIT_Q:SPLIT_V])
    for hd in range(N_KV_HEADS):
        sl = slice(hd * HEAD_DIM, (hd + 1) * HEAD_DIM)
        kn = head(zkv[:, sl], kw_ref[...])
        k_ref[:, sl] = kn.astype(BF16)
        if keep_kv:
            k32_ref[:, sl] = kn
    zv = zkv[:, KV_WIDTH:2 * KV_WIDTH]
    v_ref[...] = zv.astype(BF16)
    if keep_kv:
        v32_ref[...] = zv

    xr_ref[...] = _dot(h, w_ref[:, SPLIT_V:SPLIT_XR])
    xg_ref[...] = _dot(h, w_ref[:, SPLIT_XR:SPLIT_XG])
    g_ref[:, 0:D_MODEL] = _dot(h, w_ref[:, SPLIT_XG:SPLIT_XG + D_MODEL])
    g_ref[:, D_MODEL:] = _dot(h, w_ref[:, SPLIT_XG + D_MODEL:])


def _inproj(x, mod, l, cond_of_tile, norm1_w, w_in, q_norm_w, k_norm_w, rope_tabs, keep_kv):
    n = x.shape[0]
    tm = ROW_TILE
    rope = rope_tabs is not None
    row = lambda i: (i, 0)
    in_specs = [
        pl.BlockSpec((tm, D_MODEL), row),
        pl.BlockSpec((None, None, N_MOD, D_MODEL), lambda i: (l, cond_of_tile(i), 0, 0)),
        _resident((None, 1, D_MODEL), lambda i: (l, 0, 0)),
        _resident((None, D_MODEL, IN_WIDTH), lambda i: (l, 0, 0)),
        _resident((None, 1, HEAD_DIM), lambda i: (l, 0, 0)),
        _resident((None, 1, HEAD_DIM), lambda i: (l, 0, 0)),
    ]
    args = [x, mod, norm1_w, w_in, q_norm_w, k_norm_w]
    if rope:
        tiles_per_seq = rope_tabs[0].shape[0] // tm
        tab = pl.BlockSpec((tm, HEAD_DIM), lambda i: (i % tiles_per_seq, 0))
        in_specs += [tab, tab]
        args += list(rope_tabs)
    out_shape = [
        jax.ShapeDtypeStruct((n, ATTN_WIDTH), BF16),
        jax.ShapeDtypeStruct((n, KV_WIDTH), BF16),
        jax.ShapeDtypeStruct((n, KV_WIDTH), BF16),
        jax.ShapeDtypeStruct((n, D_RNN), F32),
        jax.ShapeDtypeStruct((n, D_RNN), F32),
        jax.ShapeDtypeStruct((n, 2 * D_MODEL), F32),
    ]
    out_specs = [
        pl.BlockSpec((tm, ATTN_WIDTH), row),
        pl.BlockSpec((tm, KV_WIDTH), row),
        pl.BlockSpec((tm, KV_WIDTH), row),
        pl.BlockSpec((tm, D_RNN), row),
        pl.BlockSpec((tm, D_RNN), row),
        pl.BlockSpec((tm, 2 * D_MODEL), row),
    ]
    if keep_kv:
        out_shape += [jax.ShapeDtypeStruct((n, KV_WIDTH), F32)] * 2
        out_specs += [pl.BlockSpec((tm, KV_WIDTH), row)] * 2
    return pl.pallas_call(
        functools.partial(_inproj_kernel, rope=rope, keep_kv=keep_kv),
        grid=(n // tm,),
        in_specs=in_specs,
        out_specs=out_specs,
        out_shape=out_shape,
        compiler_params=_params(),
        name="inproj_rope" if rope else "inproj",
    )(*args)


def _stack_heads(q_at, g):
    return jnp.concatenate(
        [q_at(slice((g * Q_PER_KV + j) * HEAD_DIM, (g * Q_PER_KV + j + 1) * HEAD_DIM))
         for j in range(Q_PER_KV)], axis=0)


def _sink_softmax_rows(s_scr, p_scr, den_scr, m_scr, sink_ref, g, rows_per_head, rch):
    c1 = SCALE * LOG2E
    cols = [slice(c * LANES, (c + 1) * LANES) for c in range(s_scr.shape[1] // LANES)]
    chunks = [(pl.ds(ci * rch, rch), g * Q_PER_KV + ci * rch // rows_per_head)
              for ci in range(Q_PER_KV * rows_per_head // rch)]
    for rws, head in chunks:
        mx = s_scr[rws, cols[0]]
        for c in cols[1:]:
            mx = jnp.maximum(mx, s_scr[rws, c])
        row_max = jnp.broadcast_to(jnp.max(mx, axis=-1, keepdims=True), mx.shape)
        m_scr[rws, :] = jnp.maximum(row_max * SCALE, sink_ref[head])
    for rws, head in chunks:
        m = m_scr[rws, :]
        mr = m * (1.0 / SCALE)
        acc = None
        for c in cols:
            e = jnp.exp2((s_scr[rws, c] - mr) * c1)
            p_scr[rws, c] = e.astype(BF16)
            acc = e if acc is None else acc + e
        row_sum = jnp.broadcast_to(jnp.sum(acc, axis=-1, keepdims=True), acc.shape)
        den_scr[rws, :] = row_sum + jnp.exp(sink_ref[head] - m)


def _ctx_attn_kernel(sink_ref, q_ref, k_ref, v_ref, o_ref, s_scr, p_scr, den_scr, m_scr):
    bt, t = q_ref.shape[0], q_ref.shape[1]

    def body(b, carry):
        for g in range(N_KV_HEADS):
            sl = slice(g * HEAD_DIM, (g + 1) * HEAD_DIM)
            q4 = _stack_heads(lambda s: q_ref[b, :, s], g)
            s_scr[g] = _dot_nt(q4, k_ref[b, :, sl])
            _sink_softmax_rows(s_scr.at[g], p_scr.at[g], den_scr.at[g], m_scr.at[g], sink_ref, g, t, ATTN_ROWS)
            o = _dot(p_scr[g], v_ref[b, :, sl]) / den_scr[g]
            for j in range(Q_PER_KV):
                hs = slice((g * Q_PER_KV + j) * HEAD_DIM, (g * Q_PER_KV + j + 1) * HEAD_DIM)
                o_ref[b, :, hs] = o[j * t:(j + 1) * t].astype(BF16)
        return carry

    lax.fori_loop(0, bt, body, 0)


def _ctx_attn(sink, q, k, v):
    b, t = q.shape[0], q.shape[1]
    bt = 4
    blk = lambda w: pl.BlockSpec((bt, t, w), lambda i: (i, 0, 0))
    return pl.pallas_call(
        _ctx_attn_kernel,
        grid=(b // bt,),
        in_specs=[pl.BlockSpec(memory_space=pltpu.SMEM), blk(ATTN_WIDTH), blk(KV_WIDTH), blk(KV_WIDTH)],
        out_specs=blk(ATTN_WIDTH),
        out_shape=jax.ShapeDtypeStruct((b, t, ATTN_WIDTH), BF16),
        scratch_shapes=[pltpu.VMEM((N_KV_HEADS, Q_PER_KV * t, t), F32),
                        pltpu.VMEM((N_KV_HEADS, Q_PER_KV * t, t), BF16),
                        pltpu.VMEM((N_KV_HEADS, Q_PER_KV * t, LANES), F32),
                        pltpu.VMEM((N_KV_HEADS, Q_PER_KV * t, LANES), F32)],
        compiler_params=_params(),
        name="ctx_attn",
    )(sink, q, k, v)


def _band_attn_kernel(sink_ref, q_ref, kp_ref, kc_ref, kn_ref, vp_ref, vc_ref, vn_ref,
                      ck_ref, cv_ref, o_ref, s_scr, p_scr, den_scr, m_scr):
    i = pl.program_id(1)
    nb = pl.num_programs(1)
    past = ck_ref.shape[0]
    rows = Q_PER_KV * BLOCK

    qrow = lax.broadcasted_iota(jnp.int32, (rows, BLOCK), 0) & (BLOCK - 1)
    col = lax.broadcasted_iota(jnp.int32, (rows, BLOCK), 1)
    in_prev = col >= qrow + jnp.where(i > 0, 0, 2 * BLOCK)
    in_next = col <= qrow - jnp.where(i < nb - 1, 0, 2 * BLOCK)

    for g in range(N_KV_HEADS):
        sl = slice(g * HEAD_DIM, (g + 1) * HEAD_DIM)
        q4 = _stack_heads(lambda s: q_ref[:, s], g)
        kloc = jnp.concatenate([kp_ref[:, sl], kc_ref[:, sl], kn_ref[:, sl]], axis=0)
        vloc = jnp.concatenate([vp_ref[:, sl], vc_ref[:, sl], vn_ref[:, sl]], axis=0)
        s_scr[g, :, 0:past] = _dot_nt(q4, ck_ref[:, sl])
        s_loc = _dot_nt(q4, kloc)
        s_scr[g, :, past:past + BLOCK] = jnp.where(in_prev, s_loc[:, 0:BLOCK], NEG)
        s_scr[g, :, past + BLOCK:past + 2 * BLOCK] = s_loc[:, BLOCK:2 * BLOCK]
        s_scr[g, :, past + 2 * BLOCK:] = jnp.where(in_next, s_loc[:, 2 * BLOCK:], NEG)
        _sink_softmax_rows(s_scr.at[g], p_scr.at[g], den_scr.at[g], m_scr.at[g], sink_ref, g, BLOCK, ATTN_ROWS)
        o = (_dot(p_scr[g, :, 0:past], cv_ref[:, sl]) + _dot(p_scr[g, :, past:], vloc)) / den_scr[g]
        for j in range(Q_PER_KV):
            hs = slice((g * Q_PER_KV + j) * HEAD_DIM, (g * Q_PER_KV + j + 1) * HEAD_DIM)
            o_ref[:, hs] = o[j * BLOCK:(j + 1) * BLOCK].astype(BF16)


def _band_attn(sink, q, k, v, ctx_k, ctx_v, l):
    b, t = q.shape[0], q.shape[1]
    nb = t // BLOCK
    past = ctx_k.shape[2]
    keys = past + 3 * BLOCK
    cur = lambda w: pl.BlockSpec((None, BLOCK, w), lambda bi, i: (bi, i, 0))
    prev = pl.BlockSpec((None, BLOCK, KV_WIDTH), lambda bi, i: (bi, jnp.maximum(i - 1, 0), 0))
    nxt = pl.BlockSpec((None, BLOCK, KV_WIDTH), lambda bi, i: (bi, jnp.minimum(i + 1, nb - 1), 0))
    ctx = pl.BlockSpec((None, None, past, KV_WIDTH), lambda bi, i: (bi, l, 0, 0))
    return pl.pallas_call(
        _band_attn_kernel,
        grid=(b, nb),
        in_specs=[pl.BlockSpec(memory_space=pltpu.SMEM), cur(ATTN_WIDTH),
                  prev, cur(KV_WIDTH), nxt, prev, cur(KV_WIDTH), nxt, ctx, ctx],
        out_specs=cur(ATTN_WIDTH),
        out_shape=jax.ShapeDtypeStruct((b, t, ATTN_WIDTH), BF16),
        scratch_shapes=[pltpu.VMEM((N_KV_HEADS, Q_PER_KV * BLOCK, keys), F32),
                        pltpu.VMEM((N_KV_HEADS, Q_PER_KV * BLOCK, keys), BF16),
                        pltpu.VMEM((N_KV_HEADS, Q_PER_KV * BLOCK, LANES), F32),
                        pltpu.VMEM((N_KV_HEADS, Q_PER_KV * BLOCK, LANES), F32)],
        compiler_params=_params(),
        name="band_attn",
    )(sink, q, k, k, k, v, v, v, ctx_k, ctx_v)


def _lru_kernel(xr_ref, xg_ref, cw_ref, cb_ref, lam_ref, wg_ref, bg_ref, h0_ref,
                y_ref, st_ref, xpad, ldec, af, bf, ab, bb, *, seq_len, seg, groups):
    t = seq_len
    n_slab = xr_ref.shape[1] // LANES
    rc = min(LRU_CHUNK, t)
    n_chunks = t // rc
    n_seg = SUBLANES * groups
    rows = n_seg * seg

    for s in range(n_slab):
        xpad[s, 0:SUBLANES, :] = jnp.zeros((SUBLANES, LANES), F32)
        xpad[s, SUBLANES + t:2 * SUBLANES + t, :] = jnp.zeros((SUBLANES, LANES), F32)
        xpad[s, SUBLANES:SUBLANES + t, :] = xr_ref[:, s * LANES:(s + 1) * LANES]
        af[s, t:rows, :] = jnp.ones((rows - t, LANES), F32)
        ab[s, t:rows, :] = jnp.ones((rows - t, LANES), F32)
        bf[s, t:rows, :] = jnp.zeros((rows - t, LANES), F32)
        bb[s, t:rows, :] = jnp.zeros((rows - t, LANES), F32)

    ldec[...] = (-0.5 * LRU_C) * _softplus(-lam_ref[...])

    def gates_chunk(idx, carry):
        sp = idx // n_chunks
        r0 = pl.multiple_of((idx - sp * n_chunks) * rc, rc)
        base = r0 + SUBLANES - CONV_LEFT
        for k in range(GATE_SLABS):
            s = sp * GATE_SLABS + k
            cw = cw_ref[s]
            xc = cw[0:1, :] * xpad[s, pl.ds(base, rc, stride=1), :]
            for tap in range(1, CONV_W):
                xc = xc + cw[tap:tap + 1, :] * xpad[s, pl.ds(base + tap, rc, stride=1), :]
            xc = xc + cb_ref[s]
            half_x = 0.5 * xc
            gh = _dot(half_x.astype(BF16), wg_ref[s]) + 0.5 * bg_ref[s]
            hd = ldec[s]
            for d, (a_scr, b_scr) in enumerate(((af, bf), (ab, bb))):
                off = 2 * d * LANES
                th_r = jnp.tanh(gh[:, off:off + LANES])
                th_i = jnp.tanh(gh[:, off + LANES:off + 2 * LANES])
                log_a = hd[d:d + 1, :] * th_r + hd[d:d + 1, :]
                a = jnp.exp(log_a)
                u = jnp.tanh(log_a) * (-1.0 - a * a)
                root = jnp.where(u == 0.0, 0.0, u * lax.rsqrt(u))
                a_scr[s, pl.ds(r0, rc), :] = a
                b_scr[s, pl.ds(r0, rc), :] = root * (half_x * th_i + half_x)
        return carry

    lax.fori_loop(0, (n_slab // GATE_SLABS) * n_chunks, gates_chunk, 0)

    def seg_rows(g, i):
        return pl.ds(g * SUBLANES * seg + i, SUBLANES, stride=seg)

    ones = jnp.ones((SUBLANES, LANES), F32)
    zeros = jnp.zeros((SUBLANES, LANES), F32)
    slabs_per_pass = max(1, SCAN_CHAINS // groups)
    chains = [(k, g) for k in range(slabs_per_pass) for g in range(groups)]

    def scan_slabs(sg, carry):
        s0 = sg * slabs_per_pass

        totals = [(ones, zeros, ones, zeros) for _ in chains]
        for i in range(seg):
            ib = seg - 1 - i
            for n, (k, g) in enumerate(chains):
                pf, hf, pb, hb = totals[n]
                a = af[s0 + k, seg_rows(g, i), :]
                a2 = ab[s0 + k, seg_rows(g, ib), :]
                totals[n] = (pf * a, a * hf + bf[s0 + k, seg_rows(g, i), :],
                             pb * a2, a2 * hb + bb[s0 + k, seg_rows(g, ib), :])

        starts = {}
        for k in range(slabs_per_pass):
            c = h0_ref[s0 + k, 0:1, :]
            for g in range(groups):
                pf, hf, _, _ = totals[k * groups + g]
                fwd = []
                for j in range(SUBLANES):
                    fwd.append(c)
                    c = hf[j:j + 1] + pf[j:j + 1] * c
                starts[(k, g, 0)] = jnp.concatenate(fwd, axis=0)
            st_ref[s0 + k, 0:1, :] = c
            c = h0_ref[s0 + k, 1:2, :]
            for g in reversed(range(groups)):
                _, _, pb, hb = totals[k * groups + g]
                bwd = [None] * SUBLANES
                for j in reversed(range(SUBLANES)):
                    bwd[j] = c
                    c = hb[j:j + 1] + pb[j:j + 1] * c
                starts[(k, g, 1)] = jnp.concatenate(bwd, axis=0)
            st_ref[s0 + k, 1:2, :] = c

        state = [(starts[(k, g, 0)], starts[(k, g, 1)]) for (k, g) in chains]
        for i in range(seg):
            ib = seg - 1 - i
            for n, (k, g) in enumerate(chains):
                hf, hb = state[n]
                hf = af[s0 + k, seg_rows(g, i), :] * hf + bf[s0 + k, seg_rows(g, i), :]
                hb = ab[s0 + k, seg_rows(g, ib), :] * hb + bb[s0 + k, seg_rows(g, ib), :]
                bf[s0 + k, seg_rows(g, i), :] = hf
                bb[s0 + k, seg_rows(g, ib), :] = hb
                state[n] = (hf, hb)
        return carry

    lax.fori_loop(0, n_slab // slabs_per_pass, scan_slabs, 0)

    def out_chunk(ci, carry):
        r0 = pl.multiple_of(ci * rc, rc)
        for s in range(n_slab):
            sl = slice(s * LANES, (s + 1) * LANES)
            hsum = bf[s, pl.ds(r0, rc), :] + bb[s, pl.ds(r0, rc), :]
            y_ref[pl.ds(r0, rc), sl] = (hsum * _gelu_tanh(xg_ref[pl.ds(r0, rc), sl])).astype(BF16)
        return carry

    lax.fori_loop(0, n_chunks, out_chunk, 0)


def _lru(xr, xg, h0, l, seq_len, conv_w, conv_b, lam, wg, bg, cb, groups):
    n = xr.shape[0]
    b = n // seq_len
    n_slab = cb // LANES
    seg = seq_len // (SUBLANES * groups) + SEG_PAD
    assert n_slab % max(GATE_SLABS, SCAN_CHAINS // groups) == 0
    rows = SUBLANES * groups * seg
    col = lambda bi, j: (bi, j)
    par = lambda r, w: pl.BlockSpec((None, n_slab, r, w), lambda bi, j: (l, j, 0, 0))
    state = pl.BlockSpec((None, n_slab, 2, LANES), lambda bi, j: (bi, j, 0, 0))
    scan_buf = pltpu.VMEM((n_slab, rows, LANES), F32)
    return pl.pallas_call(
        functools.partial(_lru_kernel, seq_len=seq_len, seg=seg, groups=groups),
        grid=(b, D_RNN // cb),
        in_specs=[
            pl.BlockSpec((seq_len, cb), col),
            pl.BlockSpec((seq_len, cb), col),
            par(CONV_W, LANES), par(1, LANES), par(2, LANES),
            par(RNN_BLOCK_W, 4 * RNN_BLOCK_W), par(1, 4 * RNN_BLOCK_W),
            state,
        ],
        out_specs=[pl.BlockSpec((seq_len, cb), col), state],
        out_shape=[
            jax.ShapeDtypeStruct((n, D_RNN), BF16),
            jax.ShapeDtypeStruct((b, N_RNN_BLOCKS, 2, LANES), F32),
        ],
        scratch_shapes=[pltpu.VMEM((n_slab, seq_len + 2 * SUBLANES, LANES), F32),
                        pltpu.VMEM((n_slab, 2, LANES), F32),
                        scan_buf, scan_buf, scan_buf, scan_buf],
        compiler_params=_params(),
        name="rglru",
    )(xr, xg, conv_w, conv_b, lam, wg, bg, h0)


def _post_kernel(x_ref, attn_ref, lru_ref, g_ref, mod_ref, nw_ref, wa_ref, wl_ref, wo_ref,
                 w1_ref, w2_ref, o_ref):
    a = _dot(attn_ref[...], wa_ref[...])
    b = _dot(lru_ref[...], wl_ref[...])
    merged = _sigmoid(g_ref[:, 0:D_MODEL]) * a + _sigmoid(g_ref[:, D_MODEL:]) * b
    x = x_ref[...] + mod_ref[2:3, :] * _dot(merged.astype(BF16), wo_ref[...])
    h = (_rms(x) * nw_ref[...] * (1.0 + mod_ref[4:5, :]) + mod_ref[3:4, :]).astype(BF16)
    ff = D_MODEL
    acc = None
    for c in range(D_FF // ff):
        hid = jnp.maximum(_dot(h, w1_ref[:, c * ff:(c + 1) * ff]), 0.0)
        part = _dot((hid * hid).astype(BF16), w2_ref[c * ff:(c + 1) * ff, :])
        acc = part if acc is None else acc + part
    o_ref[...] = x + mod_ref[5:6, :] * acc


def _post(x, attn, lru, gates, mod, l, cond_of_tile, norm2_w, w_attn_o, w_lru_o, w_out, w1, w2):
    n = x.shape[0]
    tm = ROW_TILE
    row = lambda i: (i, 0)
    sq = _resident((None, D_MODEL, D_MODEL), lambda i: (l, 0, 0))
    return pl.pallas_call(
        _post_kernel,
        grid=(n // tm,),
        in_specs=[
            pl.BlockSpec((tm, D_MODEL), row),
            pl.BlockSpec((tm, ATTN_WIDTH), row),
            pl.BlockSpec((tm, D_RNN), row),
            pl.BlockSpec((tm, 2 * D_MODEL), row),
            pl.BlockSpec((None, None, N_MOD, D_MODEL), lambda i: (l, cond_of_tile(i), 0, 0)),
            _resident((None, 1, D_MODEL), lambda i: (l, 0, 0)),
            sq, sq, sq,
            _resident((None, D_MODEL, D_FF), lambda i: (l, 0, 0)),
            _resident((None, D_FF, D_MODEL), lambda i: (l, 0, 0)),
        ],
        out_specs=pl.BlockSpec((tm, D_MODEL), row),
        out_shape=jax.ShapeDtypeStruct((n, D_MODEL), F32),
        input_output_aliases={0: 0},
        compiler_params=_params(),
        name="post",
    )(x, attn, lru, gates, mod, norm2_w, w_attn_o, w_lru_o, w_out, w1, w2)


def _rope_tables(n_tokens):
    rows = n_tokens // GRID_W
    row = jnp.broadcast_to(jnp.arange(rows)[:, None], (rows, GRID_W)).reshape(-1).astype(F32)
    col = jnp.broadcast_to(jnp.arange(GRID_W)[None, :], (rows, GRID_W)).reshape(-1).astype(F32)
    freqs = ROPE_THETA ** (-jnp.arange(0, ROPE_AXIS_DIM, 2, dtype=F32) / ROPE_AXIS_DIM)
    ang_r = row[:, None] * freqs
    ang_c = col[:, None] * freqs
    cr, sr, cc, sc = jnp.cos(ang_r), jnp.sin(ang_r), jnp.cos(ang_c), jnp.sin(ang_c)
    return (jnp.concatenate([cr, cr, cc, cc], axis=-1),
            jnp.concatenate([-sr, sr, -sc, sc], axis=-1))


def kernel(x_prompt, x_sample, cache_k, cache_v, state_lru, c, c_ctx, ada_w, ada_b, norm1_w, norm2_w, w_in, q_norm_w, k_norm_w, attn_sink, conv_w, conv_b, lru_lambda, lru_wa, lru_ba, lru_wi, lru_bi, w_attn_o, w_lru_o, w_out, mlp_w1, mlp_w2):
    bp, tp, _ = x_prompt.shape
    bs, ts, _ = x_sample.shape
    past = cache_k.shape[2]
    assert bs + 1 <= N_COND and tp == ROW_TILE and ts % ROW_TILE == 0

    cond = jnp.zeros((N_COND, D_MODEL), F32).at[0].set(c_ctx).at[1:1 + bs].set(c)
    mod = _adaln(cond, ada_w, ada_b).reshape(DEPTH, N_COND, N_MOD, D_MODEL)
    tiles_per_sample = ts // ROW_TILE
    cond_prompt = lambda i: 0
    cond_sample = lambda i: 1 + i // tiles_per_sample

    w_in_b = w_in.astype(BF16)
    w_attn_o_b = w_attn_o.astype(BF16)
    w_lru_o_b = w_lru_o.astype(BF16)
    w_out_b = w_out.astype(BF16)
    w1_b = mlp_w1.astype(BF16)
    w2_b = mlp_w2.astype(BF16)
    wg = jnp.concatenate([lru_wa[:, 0], lru_wi[:, 0], lru_wa[:, 1], lru_wi[:, 1]], axis=-1).astype(BF16)
    blk = lambda v: v.reshape(DEPTH, N_RNN_BLOCKS, 1, RNN_BLOCK_W)
    bg = jnp.concatenate([blk(lru_ba[:, 0]), blk(lru_bi[:, 0]), blk(lru_ba[:, 1]), blk(lru_bi[:, 1])], axis=-1)
    ctx_k = cache_k.reshape(bs, DEPTH, past, KV_WIDTH).astype(BF16)
    ctx_v = cache_v.reshape(bs, DEPTH, past, KV_WIDTH).astype(BF16)
    rope_tabs = _rope_tables(ts)

    n1 = norm1_w.reshape(DEPTH, 1, D_MODEL)
    n2 = norm2_w.reshape(DEPTH, 1, D_MODEL)
    qw = q_norm_w.reshape(DEPTH, 1, HEAD_DIM)
    kw = k_norm_w.reshape(DEPTH, 1, HEAD_DIM)
    by_block = lambda v: jnp.swapaxes(v.reshape(DEPTH, -1, N_RNN_BLOCKS, RNN_BLOCK_W), 1, 2)
    cvw = by_block(conv_w)
    cvb = by_block(conv_b)
    lam = by_block(lru_lambda)
    state_by_block = lambda v: jnp.swapaxes(v.reshape(-1, 2, N_RNN_BLOCKS, RNN_BLOCK_W), 1, 2)
    zero_state = jnp.zeros((bp, N_RNN_BLOCKS, 2, RNN_BLOCK_W), F32)

    xp = x_prompt.reshape(bp * tp, D_MODEL)
    xs = x_sample.reshape(bs * ts, D_MODEL)
    new_k, new_v, new_s = [], [], []
    for l in range(DEPTH):
        q, k, v, xr, xg, gates, k32, v32 = _inproj(xp, mod, l, cond_prompt, n1, w_in_b, qw, kw, None, True)
        attn = _ctx_attn(attn_sink[l], q.reshape(bp, tp, -1), k.reshape(bp, tp, -1), v.reshape(bp, tp, -1))
        lru, st = _lru(xr, xg, zero_state, l, tp, cvw, cvb, lam, wg, bg, D_RNN, 1)
        xp = _post(xp, attn.reshape(bp * tp, -1), lru, gates, mod, l, cond_prompt, n2,
                   w_attn_o_b, w_lru_o_b, w_out_b, w1_b, w2_b)
        new_k.append(k32)
        new_v.append(v32)
        new_s.append(jnp.swapaxes(st, 1, 2).reshape(bp, 2, D_RNN))
        q, k, v, xr, xg, gates = _inproj(xs, mod, l, cond_sample, n1, w_in_b, qw, kw, rope_tabs, False)
        attn = _band_attn(attn_sink[l], q.reshape(bs, ts, -1), k.reshape(bs, ts, -1),
                          v.reshape(bs, ts, -1), ctx_k, ctx_v, l)
        lru, _ = _lru(xr, xg, state_by_block(state_lru[:, l]), l, ts, cvw, cvb, lam, wg, bg, 256, 4)
        xs = _post(xs, attn.reshape(bs * ts, -1), lru, gates, mod, l, cond_sample, n2,
                   w_attn_o_b, w_lru_o_b, w_out_b, w1_b, w2_b)

    kv_shape = (bp, DEPTH, tp, N_KV_HEADS, HEAD_DIM)
    new_k_arr = jnp.stack([a.reshape(bp, tp, KV_WIDTH) for a in new_k], axis=1).reshape(kv_shape)
    new_v_arr = jnp.stack([a.reshape(bp, tp, KV_WIDTH) for a in new_v], axis=1).reshape(kv_shape)
    return (xp.reshape(bp, tp, D_MODEL), xs.reshape(bs, ts, D_MODEL), new_k_arr, new_v_arr,
            jnp.stack(new_s, axis=1))
```

```python
import functools

import jax
import jax.numpy as jnp
from jax import lax
from jax.experimental import pallas as pl
from jax.experimental.pallas import tpu as pltpu

D_MODEL = 1024
DEPTH = 4
GRID_W = 64
HEAD_DIM = 128
N_Q_HEADS = 8
N_KV_HEADS = 2
Q_PER_KV = N_Q_HEADS // N_KV_HEADS
ATTN_WIDTH = N_Q_HEADS * HEAD_DIM
KV_WIDTH = N_KV_HEADS * HEAD_DIM
WINDOW = 128
BLOCK = 128
SCALE = HEAD_DIM ** -0.5
ROPE_AXIS_DIM = HEAD_DIM // 2
ROPE_THETA = 10000.0
D_RNN = D_MODEL
N_RNN_BLOCKS = 8
RNN_BLOCK_W = D_RNN // N_RNN_BLOCKS
CONV_W = 4
CONV_LEFT = 2
LRU_C = 8.0
D_FF = 4 * D_MODEL
N_MOD = 6
EPS = 1e-6
NEG = -1e30
SPLIT_Q = ATTN_WIDTH
SPLIT_K = SPLIT_Q + KV_WIDTH
SPLIT_V = SPLIT_K + KV_WIDTH
SPLIT_XR = SPLIT_V + D_RNN
SPLIT_XG = SPLIT_XR + D_RNN
IN_WIDTH = SPLIT_XG + 2 * D_MODEL

LANES = 128
SUBLANES = 8
VMEM_LIMIT = 56 * 1024 * 1024

ROW_TILE = 256
N_COND = 8
SEG_PAD = 4
ATTN_ROWS = 64
LOG2E = 1.4426950408889634
LRU_CHUNK = 256
SCAN_CHAINS = 4
GATE_SLABS = 4

F32 = jnp.float32
BF16 = jnp.bfloat16


def _sigmoid(x):
    return 0.5 * jnp.tanh(0.5 * x) + 0.5


def _gelu_tanh(x):
    c = 0.7978845608028654
    return 0.5 * x * (1.0 + jnp.tanh(c * (x + 0.044715 * (x * x * x))))


def _softplus(x):
    return jnp.maximum(x, 0.0) + jnp.log1p(jnp.exp(-jnp.abs(x)))


def _dot(a, b):
    return jnp.dot(a, b, preferred_element_type=F32)


def _dot_nt(a, b):
    return lax.dot_general(a, b, (((1,), (1,)), ((), ())), preferred_element_type=F32)


def _rms(x):
    return x * lax.rsqrt(jnp.mean(x * x, axis=-1, keepdims=True) + EPS)


def _resident(shape, index_map):
    return pl.BlockSpec(shape, index_map, pipeline_mode=pl.Buffered(1))


def _params():
    return pltpu.CompilerParams(vmem_limit_bytes=VMEM_LIMIT)


def _adaln_kernel(cond_ref, w_ref, b_ref, o_ref):
    cnd = cond_ref[...]
    act = (cnd * _sigmoid(cnd)).astype(BF16)
    o_ref[...] = _dot(act, w_ref[...].astype(BF16)) + b_ref[...]


def _adaln(cond, ada_w, ada_b):
    tn = 1536
    width = N_MOD * D_MODEL
    return pl.pallas_call(
        _adaln_kernel,
        grid=(DEPTH, width // tn),
        in_specs=[
            pl.BlockSpec((N_COND, D_MODEL), lambda l, j: (0, 0)),
            pl.BlockSpec((None, D_MODEL, tn), lambda l, j: (l, 0, j)),
            pl.BlockSpec((None, 1, tn), lambda l, j: (l, 0, j)),
        ],
        out_specs=pl.BlockSpec((None, N_COND, tn), lambda l, j: (l, 0, j)),
        out_shape=jax.ShapeDtypeStruct((DEPTH, N_COND, width), F32),
        compiler_params=_params(),
        name="adaln",
    )(cond, ada_w, ada_b.reshape(DEPTH, 1, width))


def _rope(x, cs, sn):
    lane = lax.broadcasted_iota(jnp.int32, x.shape, 1)
    half = ROPE_AXIS_DIM // 2
    swapped = jnp.where((lane & (ROPE_AXIS_DIM - 1)) < half,
                        pltpu.roll(x, HEAD_DIM - half, 1), pltpu.roll(x, half, 1))
    return x * cs + swapped * sn


def _inproj_kernel(*refs, rope, keep_kv):
    x_ref, mod_ref, nw_ref, w_ref, qw_ref, kw_ref = refs[:6]
    refs = refs[6:]
    if rope:
        cs_ref, sn_ref = refs[:2]
        refs = refs[2:]
    if keep_kv:
        refs = refs[-8:]
        k32_ref, v32_ref = refs[6:8]
    q_ref, k_ref, v_ref, xr_ref, gx_ref, g_ref = refs[:6]

    y = _rms(x_ref[...]) * nw_ref[...]
    h = (y * (1.0 + mod_ref[1:2, :]) + mod_ref[0:1, :]).astype(BF16)

    def head(z, w):
        o = _rms(z) * w
        return _rope(o, cs_ref[...], sn_ref[...]) if rope else o

    zq = _dot(h, w_ref[:, 0:SPLIT_Q])
    for hd in range(N_Q_HEADS):
        sl = slice(hd * HEAD_DIM, (hd + 1) * HEAD_DIM)
        q_ref[:, sl] = head(zq[:, sl], qw_ref[...]).astype(BF16)

    zkv = _dot(h, w_ref[:, SPLIT_Q:SPLIT_V])
    for hd in range(N_KV_HEADS):
        sl = slice(hd * HEAD_DIM, (hd + 1) * HEAD_DIM)
        kn = head(zkv[:, sl], kw_ref[...])
        k_ref[:, sl] = kn.astype(BF16)
        if keep_kv:
            k32_ref[:, hd, :] = kn
    zv = zkv[:, KV_WIDTH:2 * KV_WIDTH]
    v_ref[...] = zv.astype(BF16)
    if keep_kv:
        for hd in range(N_KV_HEADS):
            v32_ref[:, hd, :] = zv[:, hd * HEAD_DIM:(hd + 1) * HEAD_DIM]

    xr_ref[...] = _dot(h, w_ref[:, SPLIT_V:SPLIT_XR])
    gx_ref[...] = _gelu_tanh(_dot(h, w_ref[:, SPLIT_XR:SPLIT_XG]))
    g_ref[:, 0:D_MODEL] = _dot(h, w_ref[:, SPLIT_XG:SPLIT_XG + D_MODEL])
    g_ref[:, D_MODEL:] = _dot(h, w_ref[:, SPLIT_XG + D_MODEL:])


def _inproj(x, mod, l, cond_of_tile, norm1_w, w_in, q_norm_w, k_norm_w, rope_tabs=None, kv_cache=None,
            kv_batches=None):
    n = x.shape[0]
    tm = ROW_TILE
    rope = rope_tabs is not None
    keep_kv = kv_batches is not None
    row = lambda i: (i, 0)
    in_specs = [
        pl.BlockSpec((tm, D_MODEL), row),
        pl.BlockSpec((None, None, N_MOD, D_MODEL), lambda i: (l, cond_of_tile(i), 0, 0)),
        _resident((None, 1, D_MODEL), lambda i: (l, 0, 0)),
        _resident((None, D_MODEL, IN_WIDTH), lambda i: (l, 0, 0)),
        _resident((None, 1, HEAD_DIM), lambda i: (l, 0, 0)),
        _resident((None, 1, HEAD_DIM), lambda i: (l, 0, 0)),
    ]
    args = [x, mod, norm1_w, w_in, q_norm_w, k_norm_w]
    if rope:
        tiles_per_seq = rope_tabs[0].shape[0] // tm
        tab = pl.BlockSpec((tm, HEAD_DIM), lambda i: (i % tiles_per_seq, 0))
        in_specs += [tab, tab]
        args += list(rope_tabs)
    out_shape = [
        jax.ShapeDtypeStruct((n, ATTN_WIDTH), BF16),
        jax.ShapeDtypeStruct((n, KV_WIDTH), BF16),
        jax.ShapeDtypeStruct((n, KV_WIDTH), BF16),
        jax.ShapeDtypeStruct((n, D_RNN), F32),
        jax.ShapeDtypeStruct((n, D_RNN), F32),
        jax.ShapeDtypeStruct((n, 2 * D_MODEL), F32),
    ]
    out_specs = [
        pl.BlockSpec((tm, ATTN_WIDTH), row),
        pl.BlockSpec((tm, KV_WIDTH), row),
        pl.BlockSpec((tm, KV_WIDTH), row),
        pl.BlockSpec((tm, D_RNN), row),
        pl.BlockSpec((tm, D_RNN), row),
        pl.BlockSpec((tm, 2 * D_MODEL), row),
    ]
    aliases = {}
    if keep_kv:
        kv_shape = (kv_batches, DEPTH, tm, N_KV_HEADS, HEAD_DIM)
        out_shape += [jax.ShapeDtypeStruct(kv_shape, F32)] * 2
        out_specs += [pl.BlockSpec((None, None, tm, N_KV_HEADS, HEAD_DIM), lambda i: (i, l, 0, 0, 0))] * 2
        if kv_cache is not None:
            aliases = {len(args): 6, len(args) + 1: 7}
            in_specs += [pl.BlockSpec(memory_space=pl.ANY)] * 2
            args += list(kv_cache)
    return pl.pallas_call(
        functools.partial(_inproj_kernel, rope=rope, keep_kv=keep_kv),
        grid=(n // tm,),
        in_specs=in_specs,
        out_specs=out_specs,
        out_shape=out_shape,
        input_output_aliases=aliases,
        compiler_params=_params(),
        name="inproj_rope" if rope else "inproj",
    )(*args)


def _stack_heads(q_at, g):
    return jnp.concatenate(
        [q_at(slice((g * Q_PER_KV + j) * HEAD_DIM, (g * Q_PER_KV + j + 1) * HEAD_DIM))
         for j in range(Q_PER_KV)], axis=0)


def _sink_softmax_rows(s_scr, p_scr, den_scr, m_scr, sink_ref, g, rows_per_head, rch):
    c1 = SCALE * LOG2E
    cols = [slice(c * LANES, (c + 1) * LANES) for c in range(s_scr.shape[1] // LANES)]
    chunks = [(pl.ds(ci * rch, rch), g * Q_PER_KV + ci * rch // rows_per_head)
              for ci in range(Q_PER_KV * rows_per_head // rch)]
    for rws, head in chunks:
        mx = s_scr[rws, cols[0]]
        for c in cols[1:]:
            mx = jnp.maximum(mx, s_scr[rws, c])
        row_max = jnp.broadcast_to(jnp.max(mx, axis=-1, keepdims=True), mx.shape)
        m_scr[rws, :] = jnp.maximum(row_max * SCALE, sink_ref[head])
    for rws, head in chunks:
        m = m_scr[rws, :]
        mr = m * (1.0 / SCALE)
        acc = None
        for c in cols:
            e = jnp.exp2((s_scr[rws, c] - mr) * c1)
            p_scr[rws, c] = e.astype(BF16)
            acc = e if acc is None else acc + e
        row_sum = jnp.broadcast_to(jnp.sum(acc, axis=-1, keepdims=True), acc.shape)
        den_scr[rws, :] = row_sum + jnp.exp(sink_ref[head] - m)


def _ctx_attn_kernel(sink_ref, q_ref, k_ref, v_ref, o_ref, s_scr, p_scr, den_scr, m_scr):
    bt, t = q_ref.shape[0], q_ref.shape[1]

    def body(b, carry):
        for g in range(N_KV_HEADS):
            sl = slice(g * HEAD_DIM, (g + 1) * HEAD_DIM)
            q4 = _stack_heads(lambda s: q_ref[b, :, s], g)
            s_scr[g] = _dot_nt(q4, k_ref[b, :, sl])
            _sink_softmax_rows(s_scr.at[g], p_scr.at[g], den_scr.at[g], m_scr.at[g], sink_ref, g, t, ATTN_ROWS)
            o = _dot(p_scr[g], v_ref[b, :, sl]) / den_scr[g]
            for j in range(Q_PER_KV):
                hs = slice((g * Q_PER_KV + j) * HEAD_DIM, (g * Q_PER_KV + j + 1) * HEAD_DIM)
                o_ref[b, :, hs] = o[j * t:(j + 1) * t].astype(BF16)
        return carry

    lax.fori_loop(0, bt, body, 0)


def _ctx_attn(sink, q, k, v):
    b, t = q.shape[0], q.shape[1]
    bt = 4
    blk = lambda w: pl.BlockSpec((bt, t, w), lambda i: (i, 0, 0))
    return pl.pallas_call(
        _ctx_attn_kernel,
        grid=(b // bt,),
        in_specs=[pl.BlockSpec(memory_space=pltpu.SMEM), blk(ATTN_WIDTH), blk(KV_WIDTH), blk(KV_WIDTH)],
        out_specs=blk(ATTN_WIDTH),
        out_shape=jax.ShapeDtypeStruct((b, t, ATTN_WIDTH), BF16),
        scratch_shapes=[pltpu.VMEM((N_KV_HEADS, Q_PER_KV * t, t), F32),
                        pltpu.VMEM((N_KV_HEADS, Q_PER_KV * t, t), BF16),
                        pltpu.VMEM((N_KV_HEADS, Q_PER_KV * t, LANES), F32),
                        pltpu.VMEM((N_KV_HEADS, Q_PER_KV * t, LANES), F32)],
        compiler_params=_params(),
        name="ctx_attn",
    )(sink, q, k, v)


def _band_attn_kernel(sink_ref, q_ref, kp_ref, kc_ref, kn_ref, vp_ref, vc_ref, vn_ref,
                      ck_ref, cv_ref, o_ref, s_scr, p_scr, den_scr, m_scr):
    i = pl.program_id(1)
    nb = pl.num_programs(1)
    past = ck_ref.shape[0]
    rows = Q_PER_KV * BLOCK

    qrow = lax.broadcasted_iota(jnp.int32, (rows, BLOCK), 0) & (BLOCK - 1)
    col = lax.broadcasted_iota(jnp.int32, (rows, BLOCK), 1)
    in_prev = col >= qrow + jnp.where(i > 0, 0, 2 * BLOCK)
    in_next = col <= qrow - jnp.where(i < nb - 1, 0, 2 * BLOCK)

    def scores(g):
        sl = slice(g * HEAD_DIM, (g + 1) * HEAD_DIM)
        q4 = _stack_heads(lambda s: q_ref[:, s], g)
        kloc = jnp.concatenate([kp_ref[:, sl], kc_ref[:, sl], kn_ref[:, sl]], axis=0)
        s_scr[g, :, 0:past] = _dot_nt(q4, ck_ref[:, sl])
        s_loc = _dot_nt(q4, kloc)
        s_scr[g, :, past:past + BLOCK] = jnp.where(in_prev, s_loc[:, 0:BLOCK], NEG)
        s_scr[g, :, past + BLOCK:past + 2 * BLOCK] = s_loc[:, BLOCK:2 * BLOCK]
        s_scr[g, :, past + 2 * BLOCK:] = jnp.where(in_next, s_loc[:, 2 * BLOCK:], NEG)

    def values(g):
        sl = slice(g * HEAD_DIM, (g + 1) * HEAD_DIM)
        vloc = jnp.concatenate([vp_ref[:, sl], vc_ref[:, sl], vn_ref[:, sl]], axis=0)
        o = (_dot(p_scr[g, :, 0:past], cv_ref[:, sl]) + _dot(p_scr[g, :, past:], vloc)) / den_scr[g]
        for j in range(Q_PER_KV):
            hs = slice((g * Q_PER_KV + j) * HEAD_DIM, (g * Q_PER_KV + j + 1) * HEAD_DIM)
            o_ref[:, hs] = o[j * BLOCK:(j + 1) * BLOCK].astype(BF16)

    for g in range(N_KV_HEADS):
        scores(g)
    for g in range(N_KV_HEADS):
        _sink_softmax_rows(s_scr.at[g], p_scr.at[g], den_scr.at[g], m_scr.at[g], sink_ref, g, BLOCK, ATTN_ROWS)
        values(g)


def _band_attn(sink, q, k, v, ctx_k, ctx_v, l):
    b, t = q.shape[0], q.shape[1]
    nb = t // BLOCK
    past = ctx_k.shape[2]
    keys = past + 3 * BLOCK
    cur = lambda w: pl.BlockSpec((None, BLOCK, w), lambda bi, i: (bi, i, 0))
    prev = pl.BlockSpec((None, BLOCK, KV_WIDTH), lambda bi, i: (bi, jnp.maximum(i - 1, 0), 0))
    nxt = pl.BlockSpec((None, BLOCK, KV_WIDTH), lambda bi, i: (bi, jnp.minimum(i + 1, nb - 1), 0))
    ctx = pl.BlockSpec((None, None, past, KV_WIDTH), lambda bi, i: (bi, l, 0, 0))
    return pl.pallas_call(
        _band_attn_kernel,
        grid=(b, nb),
        in_specs=[pl.BlockSpec(memory_space=pltpu.SMEM), cur(ATTN_WIDTH),
                  prev, cur(KV_WIDTH), nxt, prev, cur(KV_WIDTH), nxt, ctx, ctx],
        out_specs=cur(ATTN_WIDTH),
        out_shape=jax.ShapeDtypeStruct((b, t, ATTN_WIDTH), BF16),
        scratch_shapes=[pltpu.VMEM((N_KV_HEADS, Q_PER_KV * BLOCK, keys), F32),
                        pltpu.VMEM((N_KV_HEADS, Q_PER_KV * BLOCK, keys), BF16),
                        pltpu.VMEM((N_KV_HEADS, Q_PER_KV * BLOCK, LANES), F32),
                        pltpu.VMEM((N_KV_HEADS, Q_PER_KV * BLOCK, LANES), F32)],
        compiler_params=_params(),
        name="band_attn",
    )(sink, q, k, k, k, v, v, v, ctx_k, ctx_v)


def _lru_kernel(xr_ref, gx_ref, cw_ref, cb_ref, lam_ref, wg_ref, bg_ref, h0_ref,
                y_ref, st_ref, xpad, ldec, af, bf, ab, bb, *, seq_len, seg, groups):
    t = seq_len
    n_slab = xr_ref.shape[1] // LANES
    rc = min(LRU_CHUNK, t)
    n_chunks = t // rc
    n_seg = SUBLANES * groups
    gate_slabs = min(GATE_SLABS, n_slab)
    rows = n_seg * seg

    for s in range(n_slab):
        xpad[s, 0:SUBLANES, :] = jnp.zeros((SUBLANES, LANES), F32)
        xpad[s, SUBLANES + t:2 * SUBLANES + t, :] = jnp.zeros((SUBLANES, LANES), F32)
        xpad[s, SUBLANES:SUBLANES + t, :] = xr_ref[:, s * LANES:(s + 1) * LANES]
        af[s, t:rows, :] = jnp.ones((rows - t, LANES), F32)
        ab[s, t:rows, :] = jnp.ones((rows - t, LANES), F32)
        bf[s, t:rows, :] = jnp.zeros((rows - t, LANES), F32)
        bb[s, t:rows, :] = jnp.zeros((rows - t, LANES), F32)

    ldec[...] = (-0.5 * LRU_C) * _softplus(-lam_ref[...])

    def gates_chunk(idx, carry):
        sp = idx // n_chunks
        r0 = pl.multiple_of((idx - sp * n_chunks) * rc, rc)
        base = r0 + SUBLANES - CONV_LEFT
        for k in range(gate_slabs):
            s = sp * gate_slabs + k
            cw = cw_ref[s]
            xc = cw[0:1, :] * xpad[s, pl.ds(base, rc, stride=1), :]
            for tap in range(1, CONV_W):
                xc = xc + cw[tap:tap + 1, :] * xpad[s, pl.ds(base + tap, rc, stride=1), :]
            xc = xc + cb_ref[s]
            half_x = 0.5 * xc
            gh = _dot(half_x.astype(BF16), wg_ref[s]) + 0.5 * bg_ref[s]
            hd = ldec[s]
            for d, (a_scr, b_scr) in enumerate(((af, bf), (ab, bb))):
                off = 2 * d * LANES
                th_r = jnp.tanh(gh[:, off:off + LANES])
                th_i = jnp.tanh(gh[:, off + LANES:off + 2 * LANES])
                log_a = hd[d:d + 1, :] * th_r + hd[d:d + 1, :]
                a = jnp.exp(log_a)
                u = jnp.tanh(log_a) * (-1.0 - a * a)
                root = jnp.where(u == 0.0, 0.0, u * lax.rsqrt(u))
                a_scr[s, pl.ds(r0, rc), :] = a
                b_scr[s, pl.ds(r0, rc), :] = root * (half_x * th_i + half_x)
        return carry

    lax.fori_loop(0, (n_slab // gate_slabs) * n_chunks, gates_chunk, 0)

    def seg_rows(g, i):
        return pl.ds(g * SUBLANES * seg + i, SUBLANES, stride=seg)

    ones = jnp.ones((SUBLANES, LANES), F32)
    zeros = jnp.zeros((SUBLANES, LANES), F32)
    slabs_per_pass = max(1, SCAN_CHAINS // groups)
    chains = [(k, g) for k in range(slabs_per_pass) for g in range(groups)]

    def scan_slabs(sg, carry):
        s0 = sg * slabs_per_pass

        totals = [(ones, zeros, ones, zeros) for _ in chains]
        for i in range(seg):
            ib = seg - 1 - i
            for n, (k, g) in enumerate(chains):
                pf, hf, pb, hb = totals[n]
                a = af[s0 + k, seg_rows(g, i), :]
                a2 = ab[s0 + k, seg_rows(g, ib), :]
                totals[n] = (pf * a, a * hf + bf[s0 + k, seg_rows(g, i), :],
                             pb * a2, a2 * hb + bb[s0 + k, seg_rows(g, ib), :])

        starts = {}
        for k in range(slabs_per_pass):
            c = h0_ref[s0 + k, 0:1, :]
            for g in range(groups):
                pf, hf, _, _ = totals[k * groups + g]
                fwd = []
                for j in range(SUBLANES):
                    fwd.append(c)
                    c = hf[j:j + 1] + pf[j:j + 1] * c
                starts[(k, g, 0)] = jnp.concatenate(fwd, axis=0)
            st_ref[s0 + k, 0:1, :] = c
            c = h0_ref[s0 + k, 1:2, :]
            for g in reversed(range(groups)):
                _, _, pb, hb = totals[k * groups + g]
                bwd = [None] * SUBLANES
                for j in reversed(range(SUBLANES)):
                    bwd[j] = c
                    c = hb[j:j + 1] + pb[j:j + 1] * c
                starts[(k, g, 1)] = jnp.concatenate(bwd, axis=0)
            st_ref[s0 + k, 1:2, :] = c

        state = [(starts[(k, g, 0)], starts[(k, g, 1)]) for (k, g) in chains]
        for i in range(seg):
            ib = seg - 1 - i
            for n, (k, g) in enumerate(chains):
                hf, hb = state[n]
                hf = af[s0 + k, seg_rows(g, i), :] * hf + bf[s0 + k, seg_rows(g, i), :]
                hb = ab[s0 + k, seg_rows(g, ib), :] * hb + bb[s0 + k, seg_rows(g, ib), :]
                bf[s0 + k, seg_rows(g, i), :] = hf
                bb[s0 + k, seg_rows(g, ib), :] = hb
                state[n] = (hf, hb)
        return carry

    lax.fori_loop(0, n_slab // slabs_per_pass, scan_slabs, 0)

    def out_chunk(ci, carry):
        r0 = pl.multiple_of(ci * rc, rc)
        for s in range(n_slab):
            sl = slice(s * LANES, (s + 1) * LANES)
            hsum = bf[s, pl.ds(r0, rc), :] + bb[s, pl.ds(r0, rc), :]
            y_ref[pl.ds(r0, rc), sl] = (hsum * gx_ref[pl.ds(r0, rc), sl]).astype(BF16)
        return carry

    lax.fori_loop(0, n_chunks, out_chunk, 0)


def _lru(xr, gx, h0, l, seq_len, conv_w, conv_b, lam, wg, bg, cb, groups):
    n = xr.shape[0]
    b = n // seq_len
    n_slab = cb // LANES
    seg = seq_len // (SUBLANES * groups) + SEG_PAD
    assert n_slab % min(GATE_SLABS, n_slab) == 0 and n_slab % max(1, SCAN_CHAINS // groups) == 0
    rows = SUBLANES * groups * seg
    col = lambda bi, j: (bi, j)
    par = lambda r, w: pl.BlockSpec((None, n_slab, r, w), lambda bi, j: (l, j, 0, 0))
    state = pl.BlockSpec((None, n_slab, 2, LANES), lambda bi, j: (bi, j, 0, 0))
    scan_buf = pltpu.VMEM((n_slab, rows, LANES), F32)
    return pl.pallas_call(
        functools.partial(_lru_kernel, seq_len=seq_len, seg=seg, groups=groups),
        grid=(b, D_RNN // cb),
        in_specs=[
            pl.BlockSpec((seq_len, cb), col),
            pl.BlockSpec((seq_len, cb), col),
            par(CONV_W, LANES), par(1, LANES), par(2, LANES),
            par(RNN_BLOCK_W, 4 * RNN_BLOCK_W), par(1, 4 * RNN_BLOCK_W),
            state,
        ],
        out_specs=[pl.BlockSpec((seq_len, cb), col), state],
        out_shape=[
            jax.ShapeDtypeStruct((n, D_RNN), BF16),
            jax.ShapeDtypeStruct((b, N_RNN_BLOCKS, 2, LANES), F32),
        ],
        scratch_shapes=[pltpu.VMEM((n_slab, seq_len + 2 * SUBLANES, LANES), F32),
                        pltpu.VMEM((n_slab, 2, LANES), F32),
                        scan_buf, scan_buf, scan_buf, scan_buf],
        compiler_params=_params(),
        name="rglru",
    )(xr, gx, conv_w, conv_b, lam, wg, bg, h0)


def _post_kernel(x_ref, attn_ref, lru_ref, g_ref, mod_ref, nw_ref, wa_ref, wl_ref, wo_ref,
                 w1_ref, w2_ref, o_ref):
    a = _dot(attn_ref[...], wa_ref[...])
    b = _dot(lru_ref[...], wl_ref[...])
    merged = _sigmoid(g_ref[:, 0:D_MODEL]) * a + _sigmoid(g_ref[:, D_MODEL:]) * b
    x = x_ref[...] + mod_ref[2:3, :] * _dot(merged.astype(BF16), wo_ref[...])
    h = (_rms(x) * nw_ref[...] * (1.0 + mod_ref[4:5, :]) + mod_ref[3:4, :]).astype(BF16)
    ff = D_MODEL
    acc = None
    for c in range(D_FF // ff):
        hid = jnp.maximum(_dot(h, w1_ref[:, c * ff:(c + 1) * ff]), 0.0)
        part = _dot((hid * hid).astype(BF16), w2_ref[c * ff:(c + 1) * ff, :])
        acc = part if acc is None else acc + part
    o_ref[...] = x + mod_ref[5:6, :] * acc


def _post(x, attn, lru, gates, mod, l, cond_of_tile, norm2_w, w_attn_o, w_lru_o, w_out, w1, w2, in_place):
    n = x.shape[0]
    tm = ROW_TILE
    row = lambda i: (i, 0)
    sq = _resident((None, D_MODEL, D_MODEL), lambda i: (l, 0, 0))
    return pl.pallas_call(
        _post_kernel,
        grid=(n // tm,),
        in_specs=[
            pl.BlockSpec((tm, D_MODEL), row),
            pl.BlockSpec((tm, ATTN_WIDTH), row),
            pl.BlockSpec((tm, D_RNN), row),
            pl.BlockSpec((tm, 2 * D_MODEL), row),
            pl.BlockSpec((None, None, N_MOD, D_MODEL), lambda i: (l, cond_of_tile(i), 0, 0)),
            _resident((None, 1, D_MODEL), lambda i: (l, 0, 0)),
            sq, sq, sq,
            _resident((None, D_MODEL, D_FF), lambda i: (l, 0, 0)),
            _resident((None, D_FF, D_MODEL), lambda i: (l, 0, 0)),
        ],
        out_specs=pl.BlockSpec((tm, D_MODEL), row),
        out_shape=jax.ShapeDtypeStruct((n, D_MODEL), F32),
        input_output_aliases={0: 0} if in_place else {},
        compiler_params=_params(),
        name="post",
    )(x, attn, lru, gates, mod, norm2_w, w_attn_o, w_lru_o, w_out, w1, w2)


def _rope_tables(n_tokens):
    rows = n_tokens // GRID_W
    row = jnp.broadcast_to(jnp.arange(rows)[:, None], (rows, GRID_W)).reshape(-1).astype(F32)
    col = jnp.broadcast_to(jnp.arange(GRID_W)[None, :], (rows, GRID_W)).reshape(-1).astype(F32)
    freqs = ROPE_THETA ** (-jnp.arange(0, ROPE_AXIS_DIM, 2, dtype=F32) / ROPE_AXIS_DIM)
    ang_r = row[:, None] * freqs
    ang_c = col[:, None] * freqs
    cr, sr, cc, sc = jnp.cos(ang_r), jnp.sin(ang_r), jnp.cos(ang_c), jnp.sin(ang_c)
    return (jnp.concatenate([cr, cr, cc, cc], axis=-1),
            jnp.concatenate([-sr, sr, -sc, sc], axis=-1))


def kernel(x_prompt, x_sample, cache_k, cache_v, state_lru, c, c_ctx, ada_w, ada_b, norm1_w, norm2_w, w_in, q_norm_w, k_norm_w, attn_sink, conv_w, conv_b, lru_lambda, lru_wa, lru_ba, lru_wi, lru_bi, w_attn_o, w_lru_o, w_out, mlp_w1, mlp_w2):
    bp, tp, _ = x_prompt.shape
    bs, ts, _ = x_sample.shape
    past = cache_k.shape[2]
    assert bs + 1 <= N_COND and tp == ROW_TILE and ts % ROW_TILE == 0

    cond = jnp.zeros((N_COND, D_MODEL), F32).at[0].set(c_ctx).at[1:1 + bs].set(c)
    mod = _adaln(cond, ada_w, ada_b).reshape(DEPTH, N_COND, N_MOD, D_MODEL)
    tiles_per_sample = ts // ROW_TILE
    cond_prompt = lambda i: 0
    cond_sample = lambda i: 1 + i // tiles_per_sample

    w_in_b = w_in.astype(BF16)
    w_attn_o_b = w_attn_o.astype(BF16)
    w_lru_o_b = w_lru_o.astype(BF16)
    w_out_b = w_out.astype(BF16)
    w1_b = mlp_w1.astype(BF16)
    w2_b = mlp_w2.astype(BF16)
    wg = jnp.concatenate([lru_wa[:, 0], lru_wi[:, 0], lru_wa[:, 1], lru_wi[:, 1]], axis=-1).astype(BF16)
    blk = lambda v: v.reshape(DEPTH, N_RNN_BLOCKS, 1, RNN_BLOCK_W)
    bg = jnp.concatenate([blk(lru_ba[:, 0]), blk(lru_bi[:, 0]), blk(lru_ba[:, 1]), blk(lru_bi[:, 1])], axis=-1)
    ctx_k = cache_k.reshape(bs, DEPTH, past, KV_WIDTH).astype(BF16)
    ctx_v = cache_v.reshape(bs, DEPTH, past, KV_WIDTH).astype(BF16)
    rope_tabs = _rope_tables(ts)

    n1 = norm1_w.reshape(DEPTH, 1, D_MODEL)
    n2 = norm2_w.reshape(DEPTH, 1, D_MODEL)
    qw = q_norm_w.reshape(DEPTH, 1, HEAD_DIM)
    kw = k_norm_w.reshape(DEPTH, 1, HEAD_DIM)
    by_block = lambda v: jnp.swapaxes(v.reshape(DEPTH, -1, N_RNN_BLOCKS, RNN_BLOCK_W), 1, 2)
    cvw = by_block(conv_w)
    cvb = by_block(conv_b)
    lam = by_block(lru_lambda)
    state_by_block = lambda v: jnp.swapaxes(v.reshape(-1, 2, N_RNN_BLOCKS, RNN_BLOCK_W), 1, 2)
    zero_state = jnp.zeros((bp, N_RNN_BLOCKS, 2, RNN_BLOCK_W), F32)

    xp = x_prompt.reshape(bp * tp, D_MODEL)
    xs = x_sample.reshape(bs * ts, D_MODEL)
    kv_cache, new_s = None, []
    for l in range(DEPTH):
        in_place = l > 0
        q, k, v, xr, gx, gates, *kv_cache = _inproj(xp, mod, l, cond_prompt, n1, w_in_b, qw, kw,
                                                    kv_cache=kv_cache, kv_batches=bp)
        attn = _ctx_attn(attn_sink[l], q.reshape(bp, tp, -1), k.reshape(bp, tp, -1), v.reshape(bp, tp, -1))
        lru, st = _lru(xr, gx, zero_state, l, tp, cvw, cvb, lam, wg, bg, D_RNN, 1)
        xp = _post(xp, attn.reshape(bp * tp, -1), lru, gates, mod, l, cond_prompt, n2,
                   w_attn_o_b, w_lru_o_b, w_out_b, w1_b, w2_b, in_place)
        new_s.append(jnp.swapaxes(st, 1, 2).reshape(bp, 2, D_RNN))
        q, k, v, xr, gx, gates = _inproj(xs, mod, l, cond_sample, n1, w_in_b, qw, kw, rope_tabs=rope_tabs)
        attn = _band_attn(attn_sink[l], q.reshape(bs, ts, -1), k.reshape(bs, ts, -1),
                          v.reshape(bs, ts, -1), ctx_k, ctx_v, l)
        lru, _ = _lru(xr, gx, state_by_block(state_lru[:, l]), l, ts, cvw, cvb, lam, wg, bg, 512, 2)
        xs = _post(xs, attn.reshape(bs * ts, -1), lru, gates, mod, l, cond_sample, n2,
                   w_attn_o_b, w_lru_o_b, w_out_b, w1_b, w2_b, in_place)

    new_k_arr, new_v_arr = kv_cache
    return (xp.reshape(bp, tp, D_MODEL), xs.reshape(bs, ts, D_MODEL), new_k_arr, new_v_arr,
            jnp.stack(new_s, axis=1))
```

```python
import functools

import jax
import jax.numpy as jnp
from jax import lax
from jax.experimental import pallas as pl
from jax.experimental.pallas import tpu as pltpu

D_MODEL = 1024
DEPTH = 4
GRID_W = 64
HEAD_DIM = 128
N_Q_HEADS = 8
N_KV_HEADS = 2
Q_PER_KV = N_Q_HEADS // N_KV_HEADS
ATTN_WIDTH = N_Q_HEADS * HEAD_DIM
KV_WIDTH = N_KV_HEADS * HEAD_DIM
WINDOW = 128
BLOCK = 128
SCALE = HEAD_DIM ** -0.5
ROPE_AXIS_DIM = HEAD_DIM // 2
ROPE_THETA = 10000.0
D_RNN = D_MODEL
N_RNN_BLOCKS = 8
RNN_BLOCK_W = D_RNN // N_RNN_BLOCKS
CONV_W = 4
CONV_LEFT = 2
LRU_C = 8.0
D_FF = 4 * D_MODEL
N_MOD = 6
EPS = 1e-6
NEG = -1e30
SPLIT_Q = ATTN_WIDTH
SPLIT_K = SPLIT_Q + KV_WIDTH
SPLIT_V = SPLIT_K + KV_WIDTH
SPLIT_XR = SPLIT_V + D_RNN
SPLIT_XG = SPLIT_XR + D_RNN
IN_WIDTH = SPLIT_XG + 2 * D_MODEL

LANES = 128
SUBLANES = 8
VMEM_LIMIT = 56 * 1024 * 1024

ROW_TILE = 512
ROW_SPLIT = 2
N_COND = 8
SEG_PAD = 4
UNITS_PER_GROUP = 2
ATTN_ROWS = 64
LOG2E = 1.4426950408889634
LRU_CHUNK = 256
SCAN_CHAINS = 4
GATE_SLABS = 4

F32 = jnp.float32
BF16 = jnp.bfloat16


def _sigmoid(x):
    return 0.5 * jnp.tanh(0.5 * x) + 0.5


def _gelu_tanh(x):
    c = 0.7978845608028654
    return 0.5 * x * (1.0 + jnp.tanh(c * (x + 0.044715 * (x * x * x))))


def _softplus(x):
    return jnp.maximum(x, 0.0) + jnp.log1p(jnp.exp(-jnp.abs(x)))


def _dot(a, b):
    return jnp.dot(a, b, preferred_element_type=F32)


def _dot_nt(a, b):
    return lax.dot_general(a, b, (((1,), (1,)), ((), ())), preferred_element_type=F32)


def _rms(x):
    return x * lax.rsqrt(jnp.mean(x * x, axis=-1, keepdims=True) + EPS)


def _resident(shape, index_map):
    return pl.BlockSpec(shape, index_map, pipeline_mode=pl.Buffered(1))


def _params():
    return pltpu.CompilerParams(vmem_limit_bytes=VMEM_LIMIT)


def _adaln_kernel(cond_ref, w_ref, b_ref, o_ref):
    cnd = cond_ref[...]
    act = (cnd * _sigmoid(cnd)).astype(BF16)
    o_ref[...] = _dot(act, w_ref[...].astype(BF16)) + b_ref[...]


def _adaln(cond, ada_w, ada_b):
    tn = 1536
    width = N_MOD * D_MODEL
    return pl.pallas_call(
        _adaln_kernel,
        grid=(DEPTH, width // tn),
        in_specs=[
            pl.BlockSpec((N_COND, D_MODEL), lambda l, j: (0, 0)),
            pl.BlockSpec((None, D_MODEL, tn), lambda l, j: (l, 0, j)),
            pl.BlockSpec((None, 1, tn), lambda l, j: (l, 0, j)),
        ],
        out_specs=pl.BlockSpec((None, N_COND, tn), lambda l, j: (l, 0, j)),
        out_shape=jax.ShapeDtypeStruct((DEPTH, N_COND, width), F32),
        compiler_params=_params(),
        name="adaln",
    )(cond, ada_w, ada_b.reshape(DEPTH, 1, width))


def _rope(x, cs, sn):
    lane = lax.broadcasted_iota(jnp.int32, x.shape, 1)
    half = ROPE_AXIS_DIM // 2
    swapped = jnp.where((lane & (ROPE_AXIS_DIM - 1)) < half,
                        pltpu.roll(x, HEAD_DIM - half, 1), pltpu.roll(x, half, 1))
    return x * cs + swapped * sn


def _inproj_kernel(*refs, rope, keep_kv):
    x_ref, mod_ref, nw_ref, w_ref, qw_ref, kw_ref = refs[:6]
    refs = refs[6:]
    if rope:
        cs_ref, sn_ref = refs[:2]
        refs = refs[2:]
    if keep_kv:
        refs = refs[-8:]
        k32_ref, v32_ref = refs[6:8]
    q_ref, k_ref, v_ref, xr_ref, gx_ref, g_ref = refs[:6]

    half = x_ref.shape[0] // ROW_SPLIT
    hs = []
    for r in range(ROW_SPLIT):
        rows = slice(r * half, (r + 1) * half)
        y = _rms(x_ref[rows, :]) * nw_ref[...]
        hs.append((y * (1.0 + mod_ref[1:2, :]) + mod_ref[0:1, :]).astype(BF16))
    for r in range(ROW_SPLIT):
        rows = slice(r * half, (r + 1) * half)
        h = hs[r]

        def head(z, w):
            o = _rms(z) * w
            return _rope(o, cs_ref[rows, :], sn_ref[rows, :]) if rope else o

        zq = _dot(h, w_ref[:, 0:SPLIT_Q])
        for hd in range(N_Q_HEADS):
            sl = slice(hd * HEAD_DIM, (hd + 1) * HEAD_DIM)
            q_ref[rows, sl] = head(zq[:, sl], qw_ref[...]).astype(BF16)

        zkv = _dot(h, w_ref[:, SPLIT_Q:SPLIT_V])
        for hd in range(N_KV_HEADS):
            sl = slice(hd * HEAD_DIM, (hd + 1) * HEAD_DIM)
            kn = head(zkv[:, sl], kw_ref[...])
            k_ref[rows, sl] = kn.astype(BF16)
            if keep_kv:
                k32_ref[r, :, hd, :] = kn
        zv = zkv[:, KV_WIDTH:2 * KV_WIDTH]
        v_ref[rows, :] = zv.astype(BF16)
        if keep_kv:
            for hd in range(N_KV_HEADS):
                v32_ref[r, :, hd, :] = zv[:, hd * HEAD_DIM:(hd + 1) * HEAD_DIM]

        xr_ref[rows, :] = _dot(h, w_ref[:, SPLIT_V:SPLIT_XR])
        gx_ref[rows, :] = _gelu_tanh(_dot(h, w_ref[:, SPLIT_XR:SPLIT_XG]))
        g_ref[rows, 0:D_MODEL] = _dot(h, w_ref[:, SPLIT_XG:SPLIT_XG + D_MODEL])
        g_ref[rows, D_MODEL:] = _dot(h, w_ref[:, SPLIT_XG + D_MODEL:])


def _inproj(x, mod, l, cond_of_tile, norm1_w, w_in, q_norm_w, k_norm_w, rope_tabs=None, kv_cache=None,
            kv_batches=None):
    n = x.shape[0]
    tm = ROW_TILE
    rope = rope_tabs is not None
    keep_kv = kv_batches is not None
    row = lambda i: (i, 0)
    in_specs = [
        pl.BlockSpec((tm, D_MODEL), row),
        pl.BlockSpec((None, None, N_MOD, D_MODEL), lambda i: (l, cond_of_tile(i), 0, 0)),
        _resident((None, 1, D_MODEL), lambda i: (l, 0, 0)),
        _resident((None, D_MODEL, IN_WIDTH), lambda i: (l, 0, 0)),
        _resident((None, 1, HEAD_DIM), lambda i: (l, 0, 0)),
        _resident((None, 1, HEAD_DIM), lambda i: (l, 0, 0)),
    ]
    args = [x, mod, norm1_w, w_in, q_norm_w, k_norm_w]
    if rope:
        tiles_per_seq = rope_tabs[0].shape[0] // tm
        tab = pl.BlockSpec((tm, HEAD_DIM), lambda i: (i % tiles_per_seq, 0))
        in_specs += [tab, tab]
        args += list(rope_tabs)
    out_shape = [
        jax.ShapeDtypeStruct((n, ATTN_WIDTH), BF16),
        jax.ShapeDtypeStruct((n, KV_WIDTH), BF16),
        jax.ShapeDtypeStruct((n, KV_WIDTH), BF16),
        jax.ShapeDtypeStruct((n, D_RNN), F32),
        jax.ShapeDtypeStruct((n, D_RNN), F32),
        jax.ShapeDtypeStruct((n, 2 * D_MODEL), F32),
    ]
    out_specs = [
        pl.BlockSpec((tm, ATTN_WIDTH), row),
        pl.BlockSpec((tm, KV_WIDTH), row),
        pl.BlockSpec((tm, KV_WIDTH), row),
        pl.BlockSpec((tm, D_RNN), row),
        pl.BlockSpec((tm, D_RNN), row),
        pl.BlockSpec((tm, 2 * D_MODEL), row),
    ]
    aliases = {}
    if keep_kv:
        seq = tm // ROW_SPLIT
        assert n == kv_batches * seq
        kv_shape = (kv_batches, DEPTH, seq, N_KV_HEADS, HEAD_DIM)
        out_shape += [jax.ShapeDtypeStruct(kv_shape, F32)] * 2
        out_specs += [pl.BlockSpec((ROW_SPLIT, None, seq, N_KV_HEADS, HEAD_DIM), lambda i: (i, l, 0, 0, 0))] * 2
        if kv_cache is not None:
            aliases = {len(args): 6, len(args) + 1: 7}
            in_specs += [pl.BlockSpec(memory_space=pl.ANY)] * 2
            args += list(kv_cache)
    return pl.pallas_call(
        functools.partial(_inproj_kernel, rope=rope, keep_kv=keep_kv),
        grid=(n // tm,),
        in_specs=in_specs,
        out_specs=out_specs,
        out_shape=out_shape,
        input_output_aliases=aliases,
        compiler_params=_params(),
        name="inproj_rope" if rope else "inproj",
    )(*args)


def _stack_heads(q_at, g):
    return jnp.concatenate(
        [q_at(slice((g * Q_PER_KV + j) * HEAD_DIM, (g * Q_PER_KV + j + 1) * HEAD_DIM))
         for j in range(Q_PER_KV)], axis=0)


def _sink_softmax_rows(s_scr, p_scr, den_scr, m_scr, sink_ref, head0, rows_per_head, rch, n_heads=Q_PER_KV):
    c1 = SCALE * LOG2E
    cols = [slice(c * LANES, (c + 1) * LANES) for c in range(s_scr.shape[1] // LANES)]
    chunks = [(pl.ds(ci * rch, rch), head0 + ci * rch // rows_per_head)
              for ci in range(n_heads * rows_per_head // rch)]
    for rws, head in chunks:
        mx = s_scr[rws, cols[0]]
        for c in cols[1:]:
            mx = jnp.maximum(mx, s_scr[rws, c])
        row_max = jnp.broadcast_to(jnp.max(mx, axis=-1, keepdims=True), mx.shape)
        m_scr[rws, :] = jnp.maximum(row_max * SCALE, sink_ref[head])
    for rws, head in chunks:
        m = m_scr[rws, :]
        mr = m * (1.0 / SCALE)
        acc = None
        for c in cols:
            e = jnp.exp2((s_scr[rws, c] - mr) * c1)
            p_scr[rws, c] = e.astype(BF16)
            acc = e if acc is None else acc + e
        row_sum = jnp.broadcast_to(jnp.sum(acc, axis=-1, keepdims=True), acc.shape)
        den_scr[rws, :] = row_sum + jnp.exp(sink_ref[head] - m)


def _ctx_attn_kernel(sink_ref, q_ref, k_ref, v_ref, o_ref, s_scr, p_scr, den_scr, m_scr):
    bt, t = q_ref.shape[0], q_ref.shape[1]
    units = [(b, g) for b in range(bt) for g in range(N_KV_HEADS)]

    def scores(n, b, g):
        q4 = _stack_heads(lambda s: q_ref[b, :, s], g)
        s_scr[n % 2] = _dot_nt(q4, k_ref[b, :, g * HEAD_DIM:(g + 1) * HEAD_DIM])

    scores(0, *units[0])
    for n, (b, g) in enumerate(units):
        if n + 1 < len(units):
            scores(n + 1, *units[n + 1])
        c = n % 2
        _sink_softmax_rows(s_scr.at[c], p_scr.at[c], den_scr.at[c], m_scr.at[c], sink_ref, g * Q_PER_KV, t, ATTN_ROWS)
        o = _dot(p_scr[c], v_ref[b, :, g * HEAD_DIM:(g + 1) * HEAD_DIM]) / den_scr[c]
        for j in range(Q_PER_KV):
            hs = slice((g * Q_PER_KV + j) * HEAD_DIM, (g * Q_PER_KV + j + 1) * HEAD_DIM)
            o_ref[b, :, hs] = o[j * t:(j + 1) * t].astype(BF16)


def _ctx_attn(sink, q, k, v):
    b, t = q.shape[0], q.shape[1]
    bt = 4
    blk = lambda w: pl.BlockSpec((bt, t, w), lambda i: (i, 0, 0))
    return pl.pallas_call(
        _ctx_attn_kernel,
        grid=(b // bt,),
        in_specs=[pl.BlockSpec(memory_space=pltpu.SMEM), blk(ATTN_WIDTH), blk(KV_WIDTH), blk(KV_WIDTH)],
        out_specs=blk(ATTN_WIDTH),
        out_shape=jax.ShapeDtypeStruct((b, t, ATTN_WIDTH), BF16),
        scratch_shapes=[pltpu.VMEM((N_KV_HEADS, Q_PER_KV * t, t), F32),
                        pltpu.VMEM((N_KV_HEADS, Q_PER_KV * t, t), BF16),
                        pltpu.VMEM((N_KV_HEADS, Q_PER_KV * t, LANES), F32),
                        pltpu.VMEM((N_KV_HEADS, Q_PER_KV * t, LANES), F32)],
        compiler_params=_params(),
        name="ctx_attn",
    )(sink, q, k, v)


def _band_attn_kernel(sink_ref, q_ref, kp_ref, kc_ref, kn_ref, vp_ref, vc_ref, vn_ref,
                      ck_ref, cv_ref, o_ref, s_scr, p_scr, den_scr, m_scr):
    i = pl.program_id(1)
    nb = pl.num_programs(1)
    past = ck_ref.shape[0]
    hpu = Q_PER_KV // UNITS_PER_GROUP
    rows = hpu * BLOCK

    qrow = lax.broadcasted_iota(jnp.int32, (rows, BLOCK), 0) & (BLOCK - 1)
    col = lax.broadcasted_iota(jnp.int32, (rows, BLOCK), 1)
    in_prev = col >= qrow + jnp.where(i > 0, 0, 2 * BLOCK)
    in_next = col <= qrow - jnp.where(i < nb - 1, 0, 2 * BLOCK)
    units = [(g, u) for g in range(N_KV_HEADS) for u in range(UNITS_PER_GROUP)]

    def heads_of(g, u):
        return [g * Q_PER_KV + u * hpu + j for j in range(hpu)]

    def scores(n, g, u):
        sl = slice(g * HEAD_DIM, (g + 1) * HEAD_DIM)
        qs = jnp.concatenate([q_ref[:, h * HEAD_DIM:(h + 1) * HEAD_DIM] for h in heads_of(g, u)], axis=0)
        kloc = jnp.concatenate([kp_ref[:, sl], kc_ref[:, sl], kn_ref[:, sl]], axis=0)
        s_scr[n, :, 0:past] = _dot_nt(qs, ck_ref[:, sl])
        s_loc = _dot_nt(qs, kloc)
        s_scr[n, :, past:past + BLOCK] = jnp.where(in_prev, s_loc[:, 0:BLOCK], NEG)
        s_scr[n, :, past + BLOCK:past + 2 * BLOCK] = s_loc[:, BLOCK:2 * BLOCK]
        s_scr[n, :, past + 2 * BLOCK:] = jnp.where(in_next, s_loc[:, 2 * BLOCK:], NEG)

    def values(n, g, u):
        sl = slice(g * HEAD_DIM, (g + 1) * HEAD_DIM)
        vloc = jnp.concatenate([vp_ref[:, sl], vc_ref[:, sl], vn_ref[:, sl]], axis=0)
        o = (_dot(p_scr[n, :, 0:past], cv_ref[:, sl]) + _dot(p_scr[n, :, past:], vloc)) / den_scr[n]
        for j, h in enumerate(heads_of(g, u)):
            o_ref[:, h * HEAD_DIM:(h + 1) * HEAD_DIM] = o[j * BLOCK:(j + 1) * BLOCK].astype(BF16)

    scores(0, *units[0])
    for n, (g, u) in enumerate(units):
        if n + 1 < len(units):
            scores(n + 1, *units[n + 1])
        _sink_softmax_rows(s_scr.at[n], p_scr.at[n], den_scr.at[n], m_scr.at[n], sink_ref,
                           heads_of(g, u)[0], BLOCK, ATTN_ROWS, hpu)
        values(n, g, u)


def _band_attn(sink, q, k, v, ctx_k, ctx_v, l):
    b, t = q.shape[0], q.shape[1]
    nb = t // BLOCK
    past = ctx_k.shape[2]
    keys = past + 3 * BLOCK
    units = N_KV_HEADS * UNITS_PER_GROUP
    unit_rows = Q_PER_KV // UNITS_PER_GROUP * BLOCK
    cur = lambda w: pl.BlockSpec((None, BLOCK, w), lambda bi, i: (bi, i, 0))
    prev = pl.BlockSpec((None, BLOCK, KV_WIDTH), lambda bi, i: (bi, jnp.maximum(i - 1, 0), 0))
    nxt = pl.BlockSpec((None, BLOCK, KV_WIDTH), lambda bi, i: (bi, jnp.minimum(i + 1, nb - 1), 0))
    ctx = pl.BlockSpec((None, None, past, KV_WIDTH), lambda bi, i: (bi, l, 0, 0))
    return pl.pallas_call(
        _band_attn_kernel,
        grid=(b, nb),
        in_specs=[pl.BlockSpec(memory_space=pltpu.SMEM), cur(ATTN_WIDTH),
                  prev, cur(KV_WIDTH), nxt, prev, cur(KV_WIDTH), nxt, ctx, ctx],
        out_specs=cur(ATTN_WIDTH),
        out_shape=jax.ShapeDtypeStruct((b, t, ATTN_WIDTH), BF16),
        scratch_shapes=[pltpu.VMEM((units, unit_rows, keys), F32),
                        pltpu.VMEM((units, unit_rows, keys), BF16),
                        pltpu.VMEM((units, unit_rows, LANES), F32),
                        pltpu.VMEM((units, unit_rows, LANES), F32)],
        compiler_params=_params(),
        name="band_attn",
    )(sink, q, k, k, k, v, v, v, ctx_k, ctx_v)


def _lru_kernel(xr_ref, gx_ref, cw_ref, cb_ref, lam_ref, wg_ref, bg_ref, h0_ref,
                y_ref, st_ref, xpad, ldec, af, bf, ab, bb, *, seq_len, seg, groups):
    t = seq_len
    n_slab = xr_ref.shape[1] // LANES
    rc = min(LRU_CHUNK, t)
    n_chunks = t // rc
    n_seg = SUBLANES * groups
    gate_slabs = min(GATE_SLABS, n_slab)
    rows = n_seg * seg

    for s in range(n_slab):
        xpad[s, 0:SUBLANES, :] = jnp.zeros((SUBLANES, LANES), F32)
        xpad[s, SUBLANES + t:2 * SUBLANES + t, :] = jnp.zeros((SUBLANES, LANES), F32)
        xpad[s, SUBLANES:SUBLANES + t, :] = xr_ref[:, s * LANES:(s + 1) * LANES]
        af[s, t:rows, :] = jnp.ones((rows - t, LANES), F32)
        ab[s, t:rows, :] = jnp.ones((rows - t, LANES), F32)
        bf[s, t:rows, :] = jnp.zeros((rows - t, LANES), F32)
        bb[s, t:rows, :] = jnp.zeros((rows - t, LANES), F32)

    ldec[...] = (-0.5 * LRU_C) * _softplus(-lam_ref[...])

    def gates_chunk(idx, carry):
        sp = idx // n_chunks
        r0 = pl.multiple_of((idx - sp * n_chunks) * rc, rc)
        base = r0 + SUBLANES - CONV_LEFT
        for k in range(gate_slabs):
            s = sp * gate_slabs + k
            cw = cw_ref[s]
            xc = cw[0:1, :] * xpad[s, pl.ds(base, rc, stride=1), :]
            for tap in range(1, CONV_W):
                xc = xc + cw[tap:tap + 1, :] * xpad[s, pl.ds(base + tap, rc, stride=1), :]
            xc = xc + cb_ref[s]
            half_x = 0.5 * xc
            gh = _dot(half_x.astype(BF16), wg_ref[s]) + 0.5 * bg_ref[s]
            hd = ldec[s]
            for d, (a_scr, b_scr) in enumerate(((af, bf), (ab, bb))):
                off = 2 * d * LANES
                th_r = jnp.tanh(gh[:, off:off + LANES])
                th_i = jnp.tanh(gh[:, off + LANES:off + 2 * LANES])
                log_a = hd[d:d + 1, :] * th_r + hd[d:d + 1, :]
                a = jnp.exp(log_a)
                u = jnp.tanh(log_a) * (-1.0 - a * a)
                root = jnp.where(u == 0.0, 0.0, u * lax.rsqrt(u))
                a_scr[s, pl.ds(r0, rc), :] = a
                b_scr[s, pl.ds(r0, rc), :] = root * (half_x * th_i + half_x)
        return carry

    lax.fori_loop(0, (n_slab // gate_slabs) * n_chunks, gates_chunk, 0)

    def seg_rows(g, i):
        return pl.ds(g * SUBLANES * seg + i, SUBLANES, stride=seg)

    ones = jnp.ones((SUBLANES, LANES), F32)
    zeros = jnp.zeros((SUBLANES, LANES), F32)
    slabs_per_pass = max(1, SCAN_CHAINS // groups)
    chains = [(k, g) for k in range(slabs_per_pass) for g in range(groups)]

    def scan_slabs(sg, carry):
        s0 = sg * slabs_per_pass

        totals = [(ones, zeros, ones, zeros) for _ in chains]
        for i in range(seg):
            ib = seg - 1 - i
            for n, (k, g) in enumerate(chains):
                pf, hf, pb, hb = totals[n]
                a = af[s0 + k, seg_rows(g, i), :]
                a2 = ab[s0 + k, seg_rows(g, ib), :]
                totals[n] = (pf * a, a * hf + bf[s0 + k, seg_rows(g, i), :],
                             pb * a2, a2 * hb + bb[s0 + k, seg_rows(g, ib), :])

        starts = {}
        for k in range(slabs_per_pass):
            c = h0_ref[s0 + k, 0:1, :]
            for g in range(groups):
                pf, hf, _, _ = totals[k * groups + g]
                fwd = []
                for j in range(SUBLANES):
                    fwd.append(c)
                    c = hf[j:j + 1] + pf[j:j + 1] * c
                starts[(k, g, 0)] = jnp.concatenate(fwd, axis=0)
            st_ref[s0 + k, 0:1, :] = c
            c = h0_ref[s0 + k, 1:2, :]
            for g in reversed(range(groups)):
                _, _, pb, hb = totals[k * groups + g]
                bwd = [None] * SUBLANES
                for j in reversed(range(SUBLANES)):
                    bwd[j] = c
                    c = hb[j:j + 1] + pb[j:j + 1] * c
                starts[(k, g, 1)] = jnp.concatenate(bwd, axis=0)
            st_ref[s0 + k, 1:2, :] = c

        state = [(starts[(k, g, 0)], starts[(k, g, 1)]) for (k, g) in chains]
        for i in range(seg):
            ib = seg - 1 - i
            for n, (k, g) in enumerate(chains):
                hf, hb = state[n]
                hf = af[s0 + k, seg_rows(g, i), :] * hf + bf[s0 + k, seg_rows(g, i), :]
                hb = ab[s0 + k, seg_rows(g, ib), :] * hb + bb[s0 + k, seg_rows(g, ib), :]
                bf[s0 + k, seg_rows(g, i), :] = hf
                bb[s0 + k, seg_rows(g, ib), :] = hb
                state[n] = (hf, hb)
        return carry

    lax.fori_loop(0, n_slab // slabs_per_pass, scan_slabs, 0)

    def out_chunk(ci, carry):
        r0 = pl.multiple_of(ci * rc, rc)
        for s in range(n_slab):
            sl = slice(s * LANES, (s + 1) * LANES)
            hsum = bf[s, pl.ds(r0, rc), :] + bb[s, pl.ds(r0, rc), :]
            y_ref[pl.ds(r0, rc), sl] = (hsum * gx_ref[pl.ds(r0, rc), sl]).astype(BF16)
        return carry

    lax.fori_loop(0, n_chunks, out_chunk, 0)


def _lru(xr, gx, h0, l, seq_len, conv_w, conv_b, lam, wg, bg, cb, groups):
    n = xr.shape[0]
    b = n // seq_len
    n_slab = cb // LANES
    seg = seq_len // (SUBLANES * groups) + SEG_PAD
    assert n_slab % min(GATE_SLABS, n_slab) == 0 and n_slab % max(1, SCAN_CHAINS // groups) == 0
    rows = SUBLANES * groups * seg
    col = lambda bi, j: (bi, j)
    par = lambda r, w: pl.BlockSpec((None, n_slab, r, w), lambda bi, j: (l, j, 0, 0))
    state = pl.BlockSpec((None, n_slab, 2, LANES), lambda bi, j: (bi, j, 0, 0))
    scan_buf = pltpu.VMEM((n_slab, rows, LANES), F32)
    return pl.pallas_call(
        functools.partial(_lru_kernel, seq_len=seq_len, seg=seg, groups=groups),
        grid=(b, D_RNN // cb),
        in_specs=[
            pl.BlockSpec((seq_len, cb), col),
            pl.BlockSpec((seq_len, cb), col),
            par(CONV_W, LANES), par(1, LANES), par(2, LANES),
            par(RNN_BLOCK_W, 4 * RNN_BLOCK_W), par(1, 4 * RNN_BLOCK_W),
            state,
        ],
        out_specs=[pl.BlockSpec((seq_len, cb), col), state],
        out_shape=[
            jax.ShapeDtypeStruct((n, D_RNN), BF16),
            jax.ShapeDtypeStruct((b, N_RNN_BLOCKS, 2, LANES), F32),
        ],
        scratch_shapes=[pltpu.VMEM((n_slab, seq_len + 2 * SUBLANES, LANES), F32),
                        pltpu.VMEM((n_slab, 2, LANES), F32),
                        scan_buf, scan_buf, scan_buf, scan_buf],
        compiler_params=_params(),
        name="rglru",
    )(xr, gx, conv_w, conv_b, lam, wg, bg, h0)


def _post_kernel(x_ref, attn_ref, lru_ref, g_ref, mod_ref, nw_ref, wa_ref, wl_ref, wo_ref,
                 w1_ref, w2_ref, o_ref):
    half = x_ref.shape[0] // ROW_SPLIT
    for r in range(ROW_SPLIT):
        rows = slice(r * half, (r + 1) * half)
        a = _dot(attn_ref[rows, :], wa_ref[...])
        b = _dot(lru_ref[rows, :], wl_ref[...])
        merged = _sigmoid(g_ref[rows, 0:D_MODEL]) * a + _sigmoid(g_ref[rows, D_MODEL:]) * b
        x = x_ref[rows, :] + mod_ref[2:3, :] * _dot(merged.astype(BF16), wo_ref[...])
        h = (_rms(x) * nw_ref[...] * (1.0 + mod_ref[4:5, :]) + mod_ref[3:4, :]).astype(BF16)
        ff = D_MODEL
        acc = None
        for c in range(D_FF // ff):
            hid = jnp.maximum(_dot(h, w1_ref[:, c * ff:(c + 1) * ff]), 0.0)
            part = _dot((hid * hid).astype(BF16), w2_ref[c * ff:(c + 1) * ff, :])
            acc = part if acc is None else acc + part
        o_ref[rows, :] = x + mod_ref[5:6, :] * acc


def _post(x, attn, lru, gates, mod, l, cond_of_tile, norm2_w, w_attn_o, w_lru_o, w_out, w1, w2, in_place):
    n = x.shape[0]
    tm = ROW_TILE
    row = lambda i: (i, 0)
    sq = _resident((None, D_MODEL, D_MODEL), lambda i: (l, 0, 0))
    return pl.pallas_call(
        _post_kernel,
        grid=(n // tm,),
        in_specs=[
            pl.BlockSpec((tm, D_MODEL), row),
            pl.BlockSpec((tm, ATTN_WIDTH), row),
            pl.BlockSpec((tm, D_RNN), row),
            pl.BlockSpec((tm, 2 * D_MODEL), row),
            pl.BlockSpec((None, None, N_MOD, D_MODEL), lambda i: (l, cond_of_tile(i), 0, 0)),
            _resident((None, 1, D_MODEL), lambda i: (l, 0, 0)),
            sq, sq, sq,
            _resident((None, D_MODEL, D_FF), lambda i: (l, 0, 0)),
            _resident((None, D_FF, D_MODEL), lambda i: (l, 0, 0)),
        ],
        out_specs=pl.BlockSpec((tm, D_MODEL), row),
        out_shape=jax.ShapeDtypeStruct((n, D_MODEL), F32),
        input_output_aliases={0: 0} if in_place else {},
        compiler_params=_params(),
        name="post",
    )(x, attn, lru, gates, mod, norm2_w, w_attn_o, w_lru_o, w_out, w1, w2)


def _rope_tables(n_tokens):
    rows = n_tokens // GRID_W
    row = jnp.broadcast_to(jnp.arange(rows)[:, None], (rows, GRID_W)).reshape(-1).astype(F32)
    col = jnp.broadcast_to(jnp.arange(GRID_W)[None, :], (rows, GRID_W)).reshape(-1).astype(F32)
    freqs = ROPE_THETA ** (-jnp.arange(0, ROPE_AXIS_DIM, 2, dtype=F32) / ROPE_AXIS_DIM)
    ang_r = row[:, None] * freqs
    ang_c = col[:, None] * freqs
    cr, sr, cc, sc = jnp.cos(ang_r), jnp.sin(ang_r), jnp.cos(ang_c), jnp.sin(ang_c)
    return (jnp.concatenate([cr, cr, cc, cc], axis=-1),
            jnp.concatenate([-sr, sr, -sc, sc], axis=-1))


def kernel(x_prompt, x_sample, cache_k, cache_v, state_lru, c, c_ctx, ada_w, ada_b, norm1_w, norm2_w, w_in, q_norm_w, k_norm_w, attn_sink, conv_w, conv_b, lru_lambda, lru_wa, lru_ba, lru_wi, lru_bi, w_attn_o, w_lru_o, w_out, mlp_w1, mlp_w2):
    bp, tp, _ = x_prompt.shape
    bs, ts, _ = x_sample.shape
    past = cache_k.shape[2]
    assert bs + 1 <= N_COND and ts % ROW_TILE == 0 and (bp * tp) % ROW_TILE == 0

    cond = jnp.zeros((N_COND, D_MODEL), F32).at[0].set(c_ctx).at[1:1 + bs].set(c)
    mod = _adaln(cond, ada_w, ada_b).reshape(DEPTH, N_COND, N_MOD, D_MODEL)
    tiles_per_sample = ts // ROW_TILE
    cond_prompt = lambda i: 0
    cond_sample = lambda i: 1 + i // tiles_per_sample

    w_in_b = w_in.astype(BF16)
    w_attn_o_b = w_attn_o.astype(BF16)
    w_lru_o_b = w_lru_o.astype(BF16)
    w_out_b = w_out.astype(BF16)
    w1_b = mlp_w1.astype(BF16)
    w2_b = mlp_w2.astype(BF16)
    wg = jnp.concatenate([lru_wa[:, 0], lru_wi[:, 0], lru_wa[:, 1], lru_wi[:, 1]], axis=-1).astype(BF16)
    blk = lambda v: v.reshape(DEPTH, N_RNN_BLOCKS, 1, RNN_BLOCK_W)
    bg = jnp.concatenate([blk(lru_ba[:, 0]), blk(lru_bi[:, 0]), blk(lru_ba[:, 1]), blk(lru_bi[:, 1])], axis=-1)
    ctx_k = cache_k.reshape(bs, DEPTH, past, KV_WIDTH).astype(BF16)
    ctx_v = cache_v.reshape(bs, DEPTH, past, KV_WIDTH).astype(BF16)
    rope_tabs = _rope_tables(ts)

    n1 = norm1_w.reshape(DEPTH, 1, D_MODEL)
    n2 = norm2_w.reshape(DEPTH, 1, D_MODEL)
    qw = q_norm_w.reshape(DEPTH, 1, HEAD_DIM)
    kw = k_norm_w.reshape(DEPTH, 1, HEAD_DIM)
    by_block = lambda v: jnp.swapaxes(v.reshape(DEPTH, -1, N_RNN_BLOCKS, RNN_BLOCK_W), 1, 2)
    cvw = by_block(conv_w)
    cvb = by_block(conv_b)
    lam = by_block(lru_lambda)
    state_by_block = lambda v: jnp.swapaxes(v.reshape(-1, 2, N_RNN_BLOCKS, RNN_BLOCK_W), 1, 2)
    zero_state = jnp.zeros((bp, N_RNN_BLOCKS, 2, RNN_BLOCK_W), F32)

    xp = x_prompt.reshape(bp * tp, D_MODEL)
    xs = x_sample.reshape(bs * ts, D_MODEL)
    kv_cache, new_s = None, []
    for l in range(DEPTH):
        in_place = l > 0
        q, k, v, xr, gx, gates, *kv_cache = _inproj(xp, mod, l, cond_prompt, n1, w_in_b, qw, kw,
                                                    kv_cache=kv_cache, kv_batches=bp)
        attn = _ctx_attn(attn_sink[l], q.reshape(bp, tp, -1), k.reshape(bp, tp, -1), v.reshape(bp, tp, -1))
        lru, st = _lru(xr, gx, zero_state, l, tp, cvw, cvb, lam, wg, bg, D_RNN, 1)
        xp = _post(xp, attn.reshape(bp * tp, -1), lru, gates, mod, l, cond_prompt, n2,
                   w_attn_o_b, w_lru_o_b, w_out_b, w1_b, w2_b, in_place)
        new_s.append(jnp.swapaxes(st, 1, 2).reshape(bp, 2, D_RNN))
        q, k, v, xr, gx, gates = _inproj(xs, mod, l, cond_sample, n1, w_in_b, qw, kw, rope_tabs=rope_tabs)
        attn = _band_attn(attn_sink[l], q.reshape(bs, ts, -1), k.reshape(bs, ts, -1),
                          v.reshape(bs, ts, -1), ctx_k, ctx_v, l)
        lru, _ = _lru(xr, gx, state_by_block(state_lru[:, l]), l, ts, cvw, cvb, lam, wg, bg, 512, 2)
        xs = _post(xs, attn.reshape(bs * ts, -1), lru, gates, mod, l, cond_sample, n2,
                   w_attn_o_b, w_lru_o_b, w_out_b, w1_b, w2_b, in_place)

    new_k_arr, new_v_arr = kv_cache
    return (xp.reshape(bp, tp, D_MODEL), xs.reshape(bs, ts, D_MODEL), new_k_arr, new_v_arr,
            jnp.stack(new_s, axis=1))
```

```python
import functools

import jax
import jax.numpy as jnp
from jax import lax
from jax.experimental import pallas as pl
from jax.experimental.pallas import tpu as pltpu

D_MODEL = 1024
DEPTH = 4
GRID_W = 64
HEAD_DIM = 128
N_Q_HEADS = 8
N_KV_HEADS = 2
Q_PER_KV = N_Q_HEADS // N_KV_HEADS
ATTN_WIDTH = N_Q_HEADS * HEAD_DIM
KV_WIDTH = N_KV_HEADS * HEAD_DIM
WINDOW = 128
BLOCK = 128
SCALE = HEAD_DIM ** -0.5
ROPE_AXIS_DIM = HEAD_DIM // 2
ROPE_THETA = 10000.0
D_RNN = D_MODEL
N_RNN_BLOCKS = 8
RNN_BLOCK_W = D_RNN // N_RNN_BLOCKS
CONV_W = 4
CONV_LEFT = 2
LRU_C = 8.0
D_FF = 4 * D_MODEL
N_MOD = 6
EPS = 1e-6
NEG = -1e30
SPLIT_Q = ATTN_WIDTH
SPLIT_K = SPLIT_Q + KV_WIDTH
SPLIT_V = SPLIT_K + KV_WIDTH
SPLIT_XR = SPLIT_V + D_RNN
SPLIT_XG = SPLIT_XR + D_RNN
IN_WIDTH = SPLIT_XG + 2 * D_MODEL

LANES = 128
SUBLANES = 8
VMEM_LIMIT = 56 * 1024 * 1024

ROW_TILE = 512
ROW_SPLIT = 2
N_COND = 8
SEG_PAD = 4
ATTN_ROWS = 64
LOG2E = 1.4426950408889634
LRU_CHUNK = 256
SCAN_CHAINS = 4
GATE_SLABS = 4

F32 = jnp.float32
BF16 = jnp.bfloat16


def _sigmoid(x):
    return 0.5 * jnp.tanh(0.5 * x) + 0.5


def _gelu_tanh(x):
    c = 0.7978845608028654
    return 0.5 * x * (1.0 + jnp.tanh(c * (x + 0.044715 * (x * x * x))))


def _softplus(x):
    return jnp.maximum(x, 0.0) + jnp.log1p(jnp.exp(-jnp.abs(x)))


def _dot(a, b):
    return jnp.dot(a, b, preferred_element_type=F32)


def _dot_nt(a, b):
    return lax.dot_general(a, b, (((1,), (1,)), ((), ())), preferred_element_type=F32)


def _rms(x):
    return x * lax.rsqrt(jnp.mean(x * x, axis=-1, keepdims=True) + EPS)


def _resident(shape, index_map):
    return pl.BlockSpec(shape, index_map, pipeline_mode=pl.Buffered(1))


def _params():
    return pltpu.CompilerParams(vmem_limit_bytes=VMEM_LIMIT)


def _adaln_kernel(cond_ref, w_ref, b_ref, o_ref):
    cnd = cond_ref[...]
    act = (cnd * _sigmoid(cnd)).astype(BF16)
    o_ref[...] = _dot(act, w_ref[...].astype(BF16)) + b_ref[...]


def _adaln(cond, ada_w, ada_b):
    tn = 1536
    width = N_MOD * D_MODEL
    return pl.pallas_call(
        _adaln_kernel,
        grid=(DEPTH, width // tn),
        in_specs=[
            pl.BlockSpec((N_COND, D_MODEL), lambda l, j: (0, 0)),
            pl.BlockSpec((None, D_MODEL, tn), lambda l, j: (l, 0, j)),
            pl.BlockSpec((None, 1, tn), lambda l, j: (l, 0, j)),
        ],
        out_specs=pl.BlockSpec((None, N_COND, tn), lambda l, j: (l, 0, j)),
        out_shape=jax.ShapeDtypeStruct((DEPTH, N_COND, width), F32),
        compiler_params=_params(),
        name="adaln",
    )(cond, ada_w, ada_b.reshape(DEPTH, 1, width))


def _rope(x, cs, sn):
    lane = lax.broadcasted_iota(jnp.int32, x.shape, 1)
    half = ROPE_AXIS_DIM // 2
    swapped = jnp.where((lane & (ROPE_AXIS_DIM - 1)) < half,
                        pltpu.roll(x, HEAD_DIM - half, 1), pltpu.roll(x, half, 1))
    return x * cs + swapped * sn


def _inproj_kernel(*refs, rope, keep_kv, row_split, seq_len, slot):
    x_ref, mod_ref, nw_ref, w_ref, qw_ref, kw_ref = refs[:6]
    refs = refs[6:]
    if rope:
        cs_ref, sn_ref = refs[:2]
        refs = refs[2:]
    if keep_kv:
        refs = refs[-8:]
        k32_ref, v32_ref = refs[6:8]
    q_ref, k_ref, v_ref, xr_ref, gx_ref, g_ref = refs[:6]

    half = x_ref.shape[0] // row_split
    hs = []
    for r in range(row_split):
        rows = slice(r * half, (r + 1) * half)
        y = _rms(x_ref[rows, :]) * nw_ref[...]
        hs.append((y * (1.0 + mod_ref[1:2, :]) + mod_ref[0:1, :]).astype(BF16))
    if keep_kv and slot is not None:
        for ref in (k32_ref, v32_ref):
            for other in range(DEPTH):
                if other != slot:
                    ref[:, other] = jnp.zeros((ref.shape[0],) + ref.shape[2:], F32)
    for r in range(row_split):
        rows = slice(r * half, (r + 1) * half)
        h = hs[r]
        if keep_kv:
            kv_at = (r * half // seq_len,) + (() if slot is None else (slot,)) + (pl.ds(r * half % seq_len, half),)

        def head(z, w):
            o = _rms(z) * w
            return _rope(o, cs_ref[rows, :], sn_ref[rows, :]) if rope else o

        zq = _dot(h, w_ref[:, 0:SPLIT_Q])
        for hd in range(N_Q_HEADS):
            sl = slice(hd * HEAD_DIM, (hd + 1) * HEAD_DIM)
            q_ref[rows, sl] = head(zq[:, sl], qw_ref[...]).astype(BF16)

        zkv = _dot(h, w_ref[:, SPLIT_Q:SPLIT_V])
        for hd in range(N_KV_HEADS):
            sl = slice(hd * HEAD_DIM, (hd + 1) * HEAD_DIM)
            kn = head(zkv[:, sl], kw_ref[...])
            k_ref[rows, sl] = kn.astype(BF16)
            if keep_kv:
                k32_ref[kv_at + (hd, slice(None))] = kn
        zv = zkv[:, KV_WIDTH:2 * KV_WIDTH]
        v_ref[rows, :] = zv.astype(BF16)
        if keep_kv:
            for hd in range(N_KV_HEADS):
                v32_ref[kv_at + (hd, slice(None))] = zv[:, hd * HEAD_DIM:(hd + 1) * HEAD_DIM]

        xr_ref[rows, :] = _dot(h, w_ref[:, SPLIT_V:SPLIT_XR])
        gx_ref[rows, :] = _gelu_tanh(_dot(h, w_ref[:, SPLIT_XR:SPLIT_XG]))
        g_ref[rows, 0:D_MODEL] = _dot(h, w_ref[:, SPLIT_XG:SPLIT_XG + D_MODEL])
        g_ref[rows, D_MODEL:] = _dot(h, w_ref[:, SPLIT_XG + D_MODEL:])


def _inproj(x, mod, l, cond_of_tile, norm1_w, w_in, q_norm_w, k_norm_w, rope_tabs=None, kv_cache=None,
            kv_seq_len=None, tm=ROW_TILE):
    n = x.shape[0]
    rope = rope_tabs is not None
    keep_kv = kv_seq_len is not None
    creates = keep_kv and kv_cache is None
    row = lambda i: (i, 0)
    in_specs = [
        pl.BlockSpec((tm, D_MODEL), row),
        pl.BlockSpec((None, None, N_MOD, D_MODEL), lambda i: (l, cond_of_tile(i), 0, 0)),
        _resident((None, 1, D_MODEL), lambda i: (l, 0, 0)),
        _resident((None, D_MODEL, IN_WIDTH), lambda i: (l, 0, 0)),
        _resident((None, 1, HEAD_DIM), lambda i: (l, 0, 0)),
        _resident((None, 1, HEAD_DIM), lambda i: (l, 0, 0)),
    ]
    args = [x, mod, norm1_w, w_in, q_norm_w, k_norm_w]
    if rope:
        tiles_per_seq = rope_tabs[0].shape[0] // tm
        tab = pl.BlockSpec((tm, HEAD_DIM), lambda i: (i % tiles_per_seq, 0))
        in_specs += [tab, tab]
        args += list(rope_tabs)
    out_shape = [
        jax.ShapeDtypeStruct((n, ATTN_WIDTH), BF16),
        jax.ShapeDtypeStruct((n, KV_WIDTH), BF16),
        jax.ShapeDtypeStruct((n, KV_WIDTH), BF16),
        jax.ShapeDtypeStruct((n, D_RNN), F32),
        jax.ShapeDtypeStruct((n, D_RNN), F32),
        jax.ShapeDtypeStruct((n, 2 * D_MODEL), F32),
    ]
    out_specs = [
        pl.BlockSpec((tm, ATTN_WIDTH), row),
        pl.BlockSpec((tm, KV_WIDTH), row),
        pl.BlockSpec((tm, KV_WIDTH), row),
        pl.BlockSpec((tm, D_RNN), row),
        pl.BlockSpec((tm, D_RNN), row),
        pl.BlockSpec((tm, 2 * D_MODEL), row),
    ]
    aliases = {}
    if keep_kv:
        seqs = tm // kv_seq_len
        assert tm % kv_seq_len == 0 and n % kv_seq_len == 0
        kv_shape = (n // kv_seq_len, DEPTH, kv_seq_len, N_KV_HEADS, HEAD_DIM)
        out_shape += [jax.ShapeDtypeStruct(kv_shape, F32)] * 2
        if creates:
            kv_block = pl.BlockSpec((seqs, DEPTH, kv_seq_len, N_KV_HEADS, HEAD_DIM), lambda i: (i, 0, 0, 0, 0))
        else:
            kv_block = pl.BlockSpec((seqs, None, kv_seq_len, N_KV_HEADS, HEAD_DIM), lambda i: (i, l, 0, 0, 0))
        out_specs += [kv_block] * 2
        if kv_cache is not None:
            aliases = {len(args): 6, len(args) + 1: 7}
            in_specs += [pl.BlockSpec(memory_space=pl.ANY)] * 2
            args += list(kv_cache)
    return pl.pallas_call(
        functools.partial(_inproj_kernel, rope=rope, keep_kv=keep_kv, row_split=ROW_SPLIT,
                          seq_len=kv_seq_len, slot=l if creates else None),
        grid=(n // tm,),
        in_specs=in_specs,
        out_specs=out_specs,
        out_shape=out_shape,
        input_output_aliases=aliases,
        compiler_params=_params(),
        name="inproj_rope" if rope else "inproj",
    )(*args)


def _stack_heads(q_at, g):
    return jnp.concatenate(
        [q_at(slice((g * Q_PER_KV + j) * HEAD_DIM, (g * Q_PER_KV + j + 1) * HEAD_DIM))
         for j in range(Q_PER_KV)], axis=0)


def _sink_softmax_rows(s_scr, p_scr, den_scr, m_scr, sink_ref, head0, rows_per_head, rch, n_heads=Q_PER_KV):
    c1 = SCALE * LOG2E
    cols = [slice(c * LANES, (c + 1) * LANES) for c in range(s_scr.shape[1] // LANES)]
    chunks = [(pl.ds(ci * rch, rch), head0 + ci * rch // rows_per_head)
              for ci in range(n_heads * rows_per_head // rch)]
    for rws, head in chunks:
        mx = s_scr[rws, cols[0]]
        for c in cols[1:]:
            mx = jnp.maximum(mx, s_scr[rws, c])
        row_max = jnp.broadcast_to(jnp.max(mx, axis=-1, keepdims=True), mx.shape)
        m_scr[rws, :] = jnp.maximum(row_max * SCALE, sink_ref[head])
    for rws, head in chunks:
        m = m_scr[rws, :]
        mr = m * (1.0 / SCALE)
        acc = None
        for c in cols:
            e = jnp.exp2((s_scr[rws, c] - mr) * c1)
            p_scr[rws, c] = e.astype(BF16)
            acc = e if acc is None else acc + e
        row_sum = jnp.broadcast_to(jnp.sum(acc, axis=-1, keepdims=True), acc.shape)
        den_scr[rws, :] = row_sum + jnp.exp(sink_ref[head] - m)


def _ctx_attn_kernel(sink_ref, q_ref, k_ref, v_ref, o_ref, s_scr, p_scr, den_scr, m_scr):
    bt, t = q_ref.shape[0], q_ref.shape[1]
    units = [(b, g) for b in range(bt) for g in range(N_KV_HEADS)]

    def scores(n, b, g):
        q4 = _stack_heads(lambda s: q_ref[b, :, s], g)
        s_scr[n % 2] = _dot_nt(q4, k_ref[b, :, g * HEAD_DIM:(g + 1) * HEAD_DIM])

    scores(0, *units[0])
    for n, (b, g) in enumerate(units):
        if n + 1 < len(units):
            scores(n + 1, *units[n + 1])
        c = n % 2
        _sink_softmax_rows(s_scr.at[c], p_scr.at[c], den_scr.at[c], m_scr.at[c], sink_ref, g * Q_PER_KV, t, ATTN_ROWS)
        o = _dot(p_scr[c], v_ref[b, :, g * HEAD_DIM:(g + 1) * HEAD_DIM]) / den_scr[c]
        for j in range(Q_PER_KV):
            hs = slice((g * Q_PER_KV + j) * HEAD_DIM, (g * Q_PER_KV + j + 1) * HEAD_DIM)
            o_ref[b, :, hs] = o[j * t:(j + 1) * t].astype(BF16)


def _ctx_attn(sink, q, k, v):
    b, t = q.shape[0], q.shape[1]
    bt = 4
    blk = lambda w: pl.BlockSpec((bt, t, w), lambda i: (i, 0, 0))
    return pl.pallas_call(
        _ctx_attn_kernel,
        grid=(b // bt,),
        in_specs=[pl.BlockSpec(memory_space=pltpu.SMEM), blk(ATTN_WIDTH), blk(KV_WIDTH), blk(KV_WIDTH)],
        out_specs=blk(ATTN_WIDTH),
        out_shape=jax.ShapeDtypeStruct((b, t, ATTN_WIDTH), BF16),
        scratch_shapes=[pltpu.VMEM((N_KV_HEADS, Q_PER_KV * t, t), F32),
                        pltpu.VMEM((N_KV_HEADS, Q_PER_KV * t, t), BF16),
                        pltpu.VMEM((N_KV_HEADS, Q_PER_KV * t, LANES), F32),
                        pltpu.VMEM((N_KV_HEADS, Q_PER_KV * t, LANES), F32)],
        compiler_params=_params(),
        name="ctx_attn",
    )(sink, q, k, v)


def _band_attn_kernel(sink_ref, q_ref, kp_ref, kc_ref, kn_ref, vp_ref, vc_ref, vn_ref,
                      ck_ref, cv_ref, o_ref, s_scr, p_scr, den_scr, m_scr):
    i = pl.program_id(1)
    nb = pl.num_programs(1)
    past = ck_ref.shape[0]
    rows = Q_PER_KV * BLOCK

    qrow = lax.broadcasted_iota(jnp.int32, (rows, BLOCK), 0) & (BLOCK - 1)
    col = lax.broadcasted_iota(jnp.int32, (rows, BLOCK), 1)
    in_prev = col >= qrow + jnp.where(i > 0, 0, 2 * BLOCK)
    in_next = col <= qrow - jnp.where(i < nb - 1, 0, 2 * BLOCK)

    def scores(g):
        sl = slice(g * HEAD_DIM, (g + 1) * HEAD_DIM)
        q4 = _stack_heads(lambda s: q_ref[:, s], g)
        kloc = jnp.concatenate([kp_ref[:, sl], kc_ref[:, sl], kn_ref[:, sl]], axis=0)
        s_scr[g, :, 0:past] = _dot_nt(q4, ck_ref[:, sl])
        s_loc = _dot_nt(q4, kloc)
        s_scr[g, :, past:past + BLOCK] = jnp.where(in_prev, s_loc[:, 0:BLOCK], NEG)
        s_scr[g, :, past + BLOCK:past + 2 * BLOCK] = s_loc[:, BLOCK:2 * BLOCK]
        s_scr[g, :, past + 2 * BLOCK:] = jnp.where(in_next, s_loc[:, 2 * BLOCK:], NEG)

    def values(g):
        sl = slice(g * HEAD_DIM, (g + 1) * HEAD_DIM)
        vloc = jnp.concatenate([vp_ref[:, sl], vc_ref[:, sl], vn_ref[:, sl]], axis=0)
        o = (_dot(p_scr[g, :, 0:past], cv_ref[:, sl]) + _dot(p_scr[g, :, past:], vloc)) / den_scr[g]
        for j in range(Q_PER_KV):
            hs = slice((g * Q_PER_KV + j) * HEAD_DIM, (g * Q_PER_KV + j + 1) * HEAD_DIM)
            o_ref[:, hs] = o[j * BLOCK:(j + 1) * BLOCK].astype(BF16)

    for g in range(N_KV_HEADS):
        scores(g)
    for g in range(N_KV_HEADS):
        _sink_softmax_rows(s_scr.at[g], p_scr.at[g], den_scr.at[g], m_scr.at[g], sink_ref, g * Q_PER_KV, BLOCK, ATTN_ROWS)
        values(g)


def _band_attn(sink, q, k, v, ctx_k, ctx_v, l):
    b, t = q.shape[0], q.shape[1]
    nb = t // BLOCK
    past = ctx_k.shape[2]
    keys = past + 3 * BLOCK
    cur = lambda w: pl.BlockSpec((None, BLOCK, w), lambda bi, i: (bi, i, 0))
    prev = pl.BlockSpec((None, BLOCK, KV_WIDTH), lambda bi, i: (bi, jnp.maximum(i - 1, 0), 0))
    nxt = pl.BlockSpec((None, BLOCK, KV_WIDTH), lambda bi, i: (bi, jnp.minimum(i + 1, nb - 1), 0))
    ctx = pl.BlockSpec((None, None, past, KV_WIDTH), lambda bi, i: (bi, l, 0, 0))
    return pl.pallas_call(
        _band_attn_kernel,
        grid=(b, nb),
        in_specs=[pl.BlockSpec(memory_space=pltpu.SMEM), cur(ATTN_WIDTH),
                  prev, cur(KV_WIDTH), nxt, prev, cur(KV_WIDTH), nxt, ctx, ctx],
        out_specs=cur(ATTN_WIDTH),
        out_shape=jax.ShapeDtypeStruct((b, t, ATTN_WIDTH), BF16),
        scratch_shapes=[pltpu.VMEM((N_KV_HEADS, Q_PER_KV * BLOCK, keys), F32),
                        pltpu.VMEM((N_KV_HEADS, Q_PER_KV * BLOCK, keys), BF16),
                        pltpu.VMEM((N_KV_HEADS, Q_PER_KV * BLOCK, LANES), F32),
                        pltpu.VMEM((N_KV_HEADS, Q_PER_KV * BLOCK, LANES), F32)],
        compiler_params=_params(),
        name="band_attn",
    )(sink, q, k, k, k, v, v, v, ctx_k, ctx_v)


def _lru_kernel(xr_ref, gx_ref, cw_ref, cb_ref, lam_ref, wg_ref, bg_ref, h0_ref,
                y_ref, st_ref, xpad, ldec, af, bf, ab, bb, *, seq_len, seg, groups):
    t = seq_len
    n_slab = xr_ref.shape[1] // LANES
    rc = min(LRU_CHUNK, t)
    n_chunks = t // rc
    n_seg = SUBLANES * groups
    gate_slabs = min(GATE_SLABS, n_slab)
    rows = n_seg * seg

    for s in range(n_slab):
        xpad[s, 0:SUBLANES, :] = jnp.zeros((SUBLANES, LANES), F32)
        xpad[s, SUBLANES + t:2 * SUBLANES + t, :] = jnp.zeros((SUBLANES, LANES), F32)
        xpad[s, SUBLANES:SUBLANES + t, :] = xr_ref[:, s * LANES:(s + 1) * LANES]
        af[s, t:rows, :] = jnp.ones((rows - t, LANES), F32)
        ab[s, t:rows, :] = jnp.ones((rows - t, LANES), F32)
        bf[s, t:rows, :] = jnp.zeros((rows - t, LANES), F32)
        bb[s, t:rows, :] = jnp.zeros((rows - t, LANES), F32)

    ldec[...] = (-0.5 * LRU_C) * _softplus(-lam_ref[...])

    def gates_chunk(idx, carry):
        sp = idx // n_chunks
        r0 = pl.multiple_of((idx - sp * n_chunks) * rc, rc)
        base = r0 + SUBLANES - CONV_LEFT
        for k in range(gate_slabs):
            s = sp * gate_slabs + k
            cw = cw_ref[s]
            xc = cw[0:1, :] * xpad[s, pl.ds(base, rc, stride=1), :]
            for tap in range(1, CONV_W):
                xc = xc + cw[tap:tap + 1, :] * xpad[s, pl.ds(base + tap, rc, stride=1), :]
            xc = xc + cb_ref[s]
            half_x = 0.5 * xc
            gh = _dot(half_x.astype(BF16), wg_ref[s]) + 0.5 * bg_ref[s]
            hd = ldec[s]
            for d, (a_scr, b_scr) in enumerate(((af, bf), (ab, bb))):
                off = 2 * d * LANES
                th_r = jnp.tanh(gh[:, off:off + LANES])
                th_i = jnp.tanh(gh[:, off + LANES:off + 2 * LANES])
                log_a = hd[d:d + 1, :] * th_r + hd[d:d + 1, :]
                a = jnp.exp(log_a)
                u = jnp.tanh(log_a) * (-1.0 - a * a)
                root = jnp.where(u == 0.0, 0.0, u * lax.rsqrt(u))
                a_scr[s, pl.ds(r0, rc), :] = a
                b_scr[s, pl.ds(r0, rc), :] = root * (half_x * th_i + half_x)
        return carry

    lax.fori_loop(0, (n_slab // gate_slabs) * n_chunks, gates_chunk, 0)

    def seg_rows(g, i):
        return pl.ds(g * SUBLANES * seg + i, SUBLANES, stride=seg)

    ones = jnp.ones((SUBLANES, LANES), F32)
    zeros = jnp.zeros((SUBLANES, LANES), F32)
    slabs_per_pass = max(1, SCAN_CHAINS // groups)
    chains = [(k, g) for k in range(slabs_per_pass) for g in range(groups)]

    def scan_slabs(sg, carry):
        s0 = sg * slabs_per_pass

        totals = [(ones, zeros, ones, zeros) for _ in chains]
        for i in range(seg):
            ib = seg - 1 - i
            for n, (k, g) in enumerate(chains):
                pf, hf, pb, hb = totals[n]
                a = af[s0 + k, seg_rows(g, i), :]
                a2 = ab[s0 + k, seg_rows(g, ib), :]
                totals[n] = (pf * a, a * hf + bf[s0 + k, seg_rows(g, i), :],
                             pb * a2, a2 * hb + bb[s0 + k, seg_rows(g, ib), :])

        starts = {}
        for k in range(slabs_per_pass):
            c = h0_ref[s0 + k, 0:1, :]
            for g in range(groups):
                pf, hf, _, _ = totals[k * groups + g]
                fwd = []
                for j in range(SUBLANES):
                    fwd.append(c)
                    c = hf[j:j + 1] + pf[j:j + 1] * c
                starts[(k, g, 0)] = jnp.concatenate(fwd, axis=0)
            st_ref[s0 + k, 0:1, :] = c
            c = h0_ref[s0 + k, 1:2, :]
            for g in reversed(range(groups)):
                _, _, pb, hb = totals[k * groups + g]
                bwd = [None] * SUBLANES
                for j in reversed(range(SUBLANES)):
                    bwd[j] = c
                    c = hb[j:j + 1] + pb[j:j + 1] * c
                starts[(k, g, 1)] = jnp.concatenate(bwd, axis=0)
            st_ref[s0 + k, 1:2, :] = c

        state = [(starts[(k, g, 0)], starts[(k, g, 1)]) for (k, g) in chains]
        for i in range(seg):
            ib = seg - 1 - i
            for n, (k, g) in enumerate(chains):
                hf, hb = state[n]
                hf = af[s0 + k, seg_rows(g, i), :] * hf + bf[s0 + k, seg_rows(g, i), :]
                hb = ab[s0 + k, seg_rows(g, ib), :] * hb + bb[s0 + k, seg_rows(g, ib), :]
                bf[s0 + k, seg_rows(g, i), :] = hf
                bb[s0 + k, seg_rows(g, ib), :] = hb
                state[n] = (hf, hb)
        return carry

    lax.fori_loop(0, n_slab // slabs_per_pass, scan_slabs, 0)

    def out_chunk(ci, carry):
        r0 = pl.multiple_of(ci * rc, rc)
        for s in range(n_slab):
            sl = slice(s * LANES, (s + 1) * LANES)
            hsum = bf[s, pl.ds(r0, rc), :] + bb[s, pl.ds(r0, rc), :]
            y_ref[pl.ds(r0, rc), sl] = (hsum * gx_ref[pl.ds(r0, rc), sl]).astype(BF16)
        return carry

    lax.fori_loop(0, n_chunks, out_chunk, 0)


def _lru(xr, gx, h0, l, seq_len, conv_w, conv_b, lam, wg, bg, cb, groups):
    n = xr.shape[0]
    b = n // seq_len
    n_slab = cb // LANES
    seg = seq_len // (SUBLANES * groups) + SEG_PAD
    assert n_slab % min(GATE_SLABS, n_slab) == 0 and n_slab % max(1, SCAN_CHAINS // groups) == 0
    rows = SUBLANES * groups * seg
    col = lambda bi, j: (bi, j)
    par = lambda r, w: pl.BlockSpec((None, n_slab, r, w), lambda bi, j: (l, j, 0, 0))
    state = pl.BlockSpec((None, n_slab, 2, LANES), lambda bi, j: (bi, j, 0, 0))
    scan_buf = pltpu.VMEM((n_slab, rows, LANES), F32)
    return pl.pallas_call(
        functools.partial(_lru_kernel, seq_len=seq_len, seg=seg, groups=groups),
        grid=(b, D_RNN // cb),
        in_specs=[
            pl.BlockSpec((seq_len, cb), col),
            pl.BlockSpec((seq_len, cb), col),
            par(CONV_W, LANES), par(1, LANES), par(2, LANES),
            par(RNN_BLOCK_W, 4 * RNN_BLOCK_W), par(1, 4 * RNN_BLOCK_W),
            state,
        ],
        out_specs=[pl.BlockSpec((seq_len, cb), col), state],
        out_shape=[
            jax.ShapeDtypeStruct((n, D_RNN), BF16),
            jax.ShapeDtypeStruct((b, N_RNN_BLOCKS, 2, LANES), F32),
        ],
        scratch_shapes=[pltpu.VMEM((n_slab, seq_len + 2 * SUBLANES, LANES), F32),
                        pltpu.VMEM((n_slab, 2, LANES), F32),
                        scan_buf, scan_buf, scan_buf, scan_buf],
        compiler_params=_params(),
        name="rglru",
    )(xr, gx, conv_w, conv_b, lam, wg, bg, h0)


def _post_kernel(x_ref, attn_ref, lru_ref, g_ref, mod_ref, nw_ref, wa_ref, wl_ref, wo_ref,
                 w1_ref, w2_ref, o_ref):
    half = x_ref.shape[0] // ROW_SPLIT
    for r in range(ROW_SPLIT):
        rows = slice(r * half, (r + 1) * half)
        a = _dot(attn_ref[rows, :], wa_ref[...])
        b = _dot(lru_ref[rows, :], wl_ref[...])
        merged = _sigmoid(g_ref[rows, 0:D_MODEL]) * a + _sigmoid(g_ref[rows, D_MODEL:]) * b
        x = x_ref[rows, :] + mod_ref[2:3, :] * _dot(merged.astype(BF16), wo_ref[...])
        h = (_rms(x) * nw_ref[...] * (1.0 + mod_ref[4:5, :]) + mod_ref[3:4, :]).astype(BF16)
        ff = D_MODEL
        acc = None
        for c in range(D_FF // ff):
            hid = jnp.maximum(_dot(h, w1_ref[:, c * ff:(c + 1) * ff]), 0.0)
            part = _dot((hid * hid).astype(BF16), w2_ref[c * ff:(c + 1) * ff, :])
            acc = part if acc is None else acc + part
        o_ref[rows, :] = x + mod_ref[5:6, :] * acc


def _post(x, attn, lru, gates, mod, l, cond_of_tile, norm2_w, w_attn_o, w_lru_o, w_out, w1, w2, in_place):
    n = x.shape[0]
    tm = ROW_TILE
    row = lambda i: (i, 0)
    sq = _resident((None, D_MODEL, D_MODEL), lambda i: (l, 0, 0))
    return pl.pallas_call(
        _post_kernel,
        grid=(n // tm,),
        in_specs=[
            pl.BlockSpec((tm, D_MODEL), row),
            pl.BlockSpec((tm, ATTN_WIDTH), row),
            pl.BlockSpec((tm, D_RNN), row),
            pl.BlockSpec((tm, 2 * D_MODEL), row),
            pl.BlockSpec((None, None, N_MOD, D_MODEL), lambda i: (l, cond_of_tile(i), 0, 0)),
            _resident((None, 1, D_MODEL), lambda i: (l, 0, 0)),
            sq, sq, sq,
            _resident((None, D_MODEL, D_FF), lambda i: (l, 0, 0)),
            _resident((None, D_FF, D_MODEL), lambda i: (l, 0, 0)),
        ],
        out_specs=pl.BlockSpec((tm, D_MODEL), row),
        out_shape=jax.ShapeDtypeStruct((n, D_MODEL), F32),
        input_output_aliases={0: 0} if in_place else {},
        compiler_params=_params(),
        name="post",
    )(x, attn, lru, gates, mod, norm2_w, w_attn_o, w_lru_o, w_out, w1, w2)


def _rope_tables(n_tokens):
    rows = n_tokens // GRID_W
    row = jnp.broadcast_to(jnp.arange(rows)[:, None], (rows, GRID_W)).reshape(-1).astype(F32)
    col = jnp.broadcast_to(jnp.arange(GRID_W)[None, :], (rows, GRID_W)).reshape(-1).astype(F32)
    freqs = ROPE_THETA ** (-jnp.arange(0, ROPE_AXIS_DIM, 2, dtype=F32) / ROPE_AXIS_DIM)
    ang_r = row[:, None] * freqs
    ang_c = col[:, None] * freqs
    cr, sr, cc, sc = jnp.cos(ang_r), jnp.sin(ang_r), jnp.cos(ang_c), jnp.sin(ang_c)
    return (jnp.concatenate([cr, cr, cc, cc], axis=-1),
            jnp.concatenate([-sr, sr, -sc, sc], axis=-1))


def kernel(x_prompt, x_sample, cache_k, cache_v, state_lru, c, c_ctx, ada_w, ada_b, norm1_w, norm2_w, w_in, q_norm_w, k_norm_w, attn_sink, conv_w, conv_b, lru_lambda, lru_wa, lru_ba, lru_wi, lru_bi, w_attn_o, w_lru_o, w_out, mlp_w1, mlp_w2):
    bp, tp, _ = x_prompt.shape
    bs, ts, _ = x_sample.shape
    past = cache_k.shape[2]
    assert bs + 1 <= N_COND and ts % ROW_TILE == 0 and (bp * tp) % ROW_TILE == 0

    cond = jnp.zeros((N_COND, D_MODEL), F32).at[0].set(c_ctx).at[1:1 + bs].set(c)
    mod = _adaln(cond, ada_w, ada_b).reshape(DEPTH, N_COND, N_MOD, D_MODEL)
    tiles_per_sample = ts // ROW_TILE
    cond_prompt = lambda i: 0
    cond_sample = lambda i: 1 + i // tiles_per_sample

    w_in_b = w_in.astype(BF16)
    w_attn_o_b = w_attn_o.astype(BF16)
    w_lru_o_b = w_lru_o.astype(BF16)
    w_out_b = w_out.astype(BF16)
    w1_b = mlp_w1.astype(BF16)
    w2_b = mlp_w2.astype(BF16)
    wg = jnp.concatenate([lru_wa[:, 0], lru_wi[:, 0], lru_wa[:, 1], lru_wi[:, 1]], axis=-1).astype(BF16)
    blk = lambda v: v.reshape(DEPTH, N_RNN_BLOCKS, 1, RNN_BLOCK_W)
    bg = jnp.concatenate([blk(lru_ba[:, 0]), blk(lru_bi[:, 0]), blk(lru_ba[:, 1]), blk(lru_bi[:, 1])], axis=-1)
    ctx_k = cache_k.reshape(bs, DEPTH, past, KV_WIDTH).astype(BF16)
    ctx_v = cache_v.reshape(bs, DEPTH, past, KV_WIDTH).astype(BF16)
    rope_tabs = _rope_tables(ts)

    n1 = norm1_w.reshape(DEPTH, 1, D_MODEL)
    n2 = norm2_w.reshape(DEPTH, 1, D_MODEL)
    qw = q_norm_w.reshape(DEPTH, 1, HEAD_DIM)
    kw = k_norm_w.reshape(DEPTH, 1, HEAD_DIM)
    by_block = lambda v: jnp.swapaxes(v.reshape(DEPTH, -1, N_RNN_BLOCKS, RNN_BLOCK_W), 1, 2)
    cvw = by_block(conv_w)
    cvb = by_block(conv_b)
    lam = by_block(lru_lambda)
    state_by_block = lambda v: jnp.swapaxes(v.reshape(-1, 2, N_RNN_BLOCKS, RNN_BLOCK_W), 1, 2)
    zero_state = jnp.zeros((bp, N_RNN_BLOCKS, 2, RNN_BLOCK_W), F32)

    xp = x_prompt.reshape(bp * tp, D_MODEL)
    xs = x_sample.reshape(bs * ts, D_MODEL)
    kv_cache, new_s = None, []
    for l in range(DEPTH):
        in_place = l > 0
        q, k, v, xr, gx, gates, *kv_cache = _inproj(xp, mod, l, cond_prompt, n1, w_in_b, qw, kw, kv_cache=kv_cache,
                                                    kv_seq_len=tp, tm=ROW_TILE if kv_cache else tp)
        attn = _ctx_attn(attn_sink[l], q.reshape(bp, tp, -1), k.reshape(bp, tp, -1), v.reshape(bp, tp, -1))
        lru, st = _lru(xr, gx, zero_state, l, tp, cvw, cvb, lam, wg, bg, D_RNN, 1)
        xp = _post(xp, attn.reshape(bp * tp, -1), lru, gates, mod, l, cond_prompt, n2,
                   w_attn_o_b, w_lru_o_b, w_out_b, w1_b, w2_b, in_place)
        new_s.append(jnp.swapaxes(st, 1, 2).reshape(bp, 2, D_RNN))
        q, k, v, xr, gx, gates = _inproj(xs, mod, l, cond_sample, n1, w_in_b, qw, kw, rope_tabs=rope_tabs)
        attn = _band_attn(attn_sink[l], q.reshape(bs, ts, -1), k.reshape(bs, ts, -1),
                          v.reshape(bs, ts, -1), ctx_k, ctx_v, l)
        lru, _ = _lru(xr, gx, state_by_block(state_lru[:, l]), l, ts, cvw, cvb, lam, wg, bg, 512, 2)
        xs = _post(xs, attn.reshape(bs * ts, -1), lru, gates, mod, l, cond_sample, n2,
                   w_attn_o_b, w_lru_o_b, w_out_b, w1_b, w2_b, in_place)

    new_k_arr, new_v_arr = kv_cache
    return (xp.reshape(bp, tp, D_MODEL), xs.reshape(bs, ts, D_MODEL), new_k_arr, new_v_arr,
            jnp.stack(new_s, axis=1))
```

```python
import functools

import jax
import jax.numpy as jnp
from jax import lax
from jax.experimental import pallas as pl
from jax.experimental.pallas import tpu as pltpu

D_MODEL = 1024
DEPTH = 4
GRID_W = 64
HEAD_DIM = 128
N_Q_HEADS = 8
N_KV_HEADS = 2
Q_PER_KV = N_Q_HEADS // N_KV_HEADS
ATTN_WIDTH = N_Q_HEADS * HEAD_DIM
KV_WIDTH = N_KV_HEADS * HEAD_DIM
WINDOW = 128
BLOCK = 128
SCALE = HEAD_DIM ** -0.5
ROPE_AXIS_DIM = HEAD_DIM // 2
ROPE_THETA = 10000.0
D_RNN = D_MODEL
N_RNN_BLOCKS = 8
RNN_BLOCK_W = D_RNN // N_RNN_BLOCKS
CONV_W = 4
CONV_LEFT = 2
LRU_C = 8.0
D_FF = 4 * D_MODEL
N_MOD = 6
EPS = 1e-6
NEG = -1e30
SPLIT_Q = ATTN_WIDTH
SPLIT_K = SPLIT_Q + KV_WIDTH
SPLIT_V = SPLIT_K + KV_WIDTH
SPLIT_XR = SPLIT_V + D_RNN
SPLIT_XG = SPLIT_XR + D_RNN
IN_WIDTH = SPLIT_XG + 2 * D_MODEL

LANES = 128
SUBLANES = 8
VMEM_LIMIT = 56 * 1024 * 1024

ROW_TILE = 512
ROW_SPLIT = 2
N_COND = 8
SEG_PAD = 4
ATTN_ROWS = 64
LOG2E = 1.4426950408889634
LRU_CHUNK = 256
SCAN_CHAINS = 4
GATE_SLABS = 4

F32 = jnp.float32
BF16 = jnp.bfloat16


def _sigmoid(x):
    return 0.5 * jnp.tanh(0.5 * x) + 0.5


def _gelu_tanh(x):
    c = 0.7978845608028654
    return 0.5 * x * (1.0 + jnp.tanh(c * (x + 0.044715 * (x * x * x))))


def _softplus(x):
    return jnp.maximum(x, 0.0) + jnp.log1p(jnp.exp(-jnp.abs(x)))


def _dot(a, b):
    return jnp.dot(a, b, preferred_element_type=F32)


def _dot_nt(a, b):
    return lax.dot_general(a, b, (((1,), (1,)), ((), ())), preferred_element_type=F32)


def _rms(x):
    return x * lax.rsqrt(jnp.mean(x * x, axis=-1, keepdims=True) + EPS)


def _resident(shape, index_map):
    return pl.BlockSpec(shape, index_map, pipeline_mode=pl.Buffered(1))


def _layer_weight(w, l):
    if w.ndim == 3:
        return _resident((None,) + w.shape[1:], lambda i: (l, 0, 0))
    return _resident(w.shape, lambda i: (0, 0))


def _params():
    return pltpu.CompilerParams(vmem_limit_bytes=VMEM_LIMIT)


def _adaln_kernel(cond_ref, w_ref, b_ref, o_ref):
    cnd = cond_ref[...]
    act = (cnd * _sigmoid(cnd)).astype(BF16)
    o_ref[...] = _dot(act, w_ref[...].astype(BF16)) + b_ref[...]


def _adaln(cond, ada_w, ada_b):
    tn = 1536
    width = N_MOD * D_MODEL
    return pl.pallas_call(
        _adaln_kernel,
        grid=(DEPTH, width // tn),
        in_specs=[
            pl.BlockSpec((N_COND, D_MODEL), lambda l, j: (0, 0)),
            pl.BlockSpec((None, D_MODEL, tn), lambda l, j: (l, 0, j)),
            pl.BlockSpec((None, 1, tn), lambda l, j: (l, 0, j)),
        ],
        out_specs=pl.BlockSpec((None, N_COND, tn), lambda l, j: (l, 0, j)),
        out_shape=jax.ShapeDtypeStruct((DEPTH, N_COND, width), F32),
        compiler_params=_params(),
        name="adaln",
    )(cond, ada_w, ada_b.reshape(DEPTH, 1, width))


def _rope(x, cs, sn):
    lane = lax.broadcasted_iota(jnp.int32, x.shape, 1)
    half = ROPE_AXIS_DIM // 2
    swapped = jnp.where((lane & (ROPE_AXIS_DIM - 1)) < half,
                        pltpu.roll(x, HEAD_DIM - half, 1), pltpu.roll(x, half, 1))
    return x * cs + swapped * sn


def _inproj_kernel(*refs, rope, keep_kv, row_split, seq_len, slot):
    x_ref, mod_ref, nw_ref, w_ref, qw_ref, kw_ref = refs[:6]
    refs = refs[6:]
    if rope:
        cs_ref, sn_ref = refs[:2]
        refs = refs[2:]
    if keep_kv:
        refs = refs[-8:]
        k32_ref, v32_ref = refs[6:8]
    q_ref, k_ref, v_ref, xr_ref, gx_ref, g_ref = refs[:6]

    half = x_ref.shape[0] // row_split
    hs = []
    for r in range(row_split):
        rows = slice(r * half, (r + 1) * half)
        y = _rms(x_ref[rows, :]) * nw_ref[...]
        hs.append((y * (1.0 + mod_ref[1:2, :]) + mod_ref[0:1, :]).astype(BF16))
    if keep_kv and slot is not None:
        for ref in (k32_ref, v32_ref):
            for other in range(DEPTH):
                if other != slot:
                    ref[:, other] = jnp.zeros((ref.shape[0],) + ref.shape[2:], F32)
    for r in range(row_split):
        rows = slice(r * half, (r + 1) * half)
        h = hs[r]
        if keep_kv:
            kv_at = (r * half // seq_len,) + (() if slot is None else (slot,)) + (pl.ds(r * half % seq_len, half),)

        def head(z, w):
            o = _rms(z) * w
            return _rope(o, cs_ref[rows, :], sn_ref[rows, :]) if rope else o

        zq = _dot(h, w_ref[:, 0:SPLIT_Q])
        for hd in range(N_Q_HEADS):
            sl = slice(hd * HEAD_DIM, (hd + 1) * HEAD_DIM)
            q_ref[rows, sl] = head(zq[:, sl], qw_ref[...]).astype(BF16)

        zkv = _dot(h, w_ref[:, SPLIT_Q:SPLIT_V])
        for hd in range(N_KV_HEADS):
            sl = slice(hd * HEAD_DIM, (hd + 1) * HEAD_DIM)
            kn = head(zkv[:, sl], kw_ref[...])
            k_ref[rows, sl] = kn.astype(BF16)
            if keep_kv:
                k32_ref[kv_at + (hd, slice(None))] = kn
        zv = zkv[:, KV_WIDTH:2 * KV_WIDTH]
        v_ref[rows, :] = zv.astype(BF16)
        if keep_kv:
            for hd in range(N_KV_HEADS):
                v32_ref[kv_at + (hd, slice(None))] = zv[:, hd * HEAD_DIM:(hd + 1) * HEAD_DIM]

        xr_ref[rows, :] = _dot(h, w_ref[:, SPLIT_V:SPLIT_XR])
        gx_ref[rows, :] = _gelu_tanh(_dot(h, w_ref[:, SPLIT_XR:SPLIT_XG]))
        g_ref[rows, 0:D_MODEL] = _dot(h, w_ref[:, SPLIT_XG:SPLIT_XG + D_MODEL])
        g_ref[rows, D_MODEL:] = _dot(h, w_ref[:, SPLIT_XG + D_MODEL:])


def _inproj(x, mod, l, cond_of_tile, norm1_w, w_in, q_norm_w, k_norm_w, rope_tabs=None, kv_cache=None,
            kv_seq_len=None, tm=ROW_TILE):
    n = x.shape[0]
    rope = rope_tabs is not None
    keep_kv = kv_seq_len is not None
    creates = keep_kv and kv_cache is None
    row = lambda i: (i, 0)
    in_specs = [
        pl.BlockSpec((tm, D_MODEL), row),
        pl.BlockSpec((None, None, N_MOD, D_MODEL), lambda i: (l, cond_of_tile(i), 0, 0)),
        _resident((None, 1, D_MODEL), lambda i: (l, 0, 0)),
        _layer_weight(w_in, l),
        _resident((None, 1, HEAD_DIM), lambda i: (l, 0, 0)),
        _resident((None, 1, HEAD_DIM), lambda i: (l, 0, 0)),
    ]
    args = [x, mod, norm1_w, w_in, q_norm_w, k_norm_w]
    if rope:
        tiles_per_seq = rope_tabs[0].shape[0] // tm
        tab = pl.BlockSpec((tm, HEAD_DIM), lambda i: (i % tiles_per_seq, 0))
        in_specs += [tab, tab]
        args += list(rope_tabs)
    out_shape = [
        jax.ShapeDtypeStruct((n, ATTN_WIDTH), BF16),
        jax.ShapeDtypeStruct((n, KV_WIDTH), BF16),
        jax.ShapeDtypeStruct((n, KV_WIDTH), BF16),
        jax.ShapeDtypeStruct((n, D_RNN), F32),
        jax.ShapeDtypeStruct((n, D_RNN), F32),
        jax.ShapeDtypeStruct((n, 2 * D_MODEL), F32),
    ]
    out_specs = [
        pl.BlockSpec((tm, ATTN_WIDTH), row),
        pl.BlockSpec((tm, KV_WIDTH), row),
        pl.BlockSpec((tm, KV_WIDTH), row),
        pl.BlockSpec((tm, D_RNN), row),
        pl.BlockSpec((tm, D_RNN), row),
        pl.BlockSpec((tm, 2 * D_MODEL), row),
    ]
    aliases = {}
    if keep_kv:
        seqs = tm // kv_seq_len
        assert tm % kv_seq_len == 0 and n % kv_seq_len == 0
        kv_shape = (n // kv_seq_len, DEPTH, kv_seq_len, N_KV_HEADS, HEAD_DIM)
        out_shape += [jax.ShapeDtypeStruct(kv_shape, F32)] * 2
        if creates:
            kv_block = pl.BlockSpec((seqs, DEPTH, kv_seq_len, N_KV_HEADS, HEAD_DIM), lambda i: (i, 0, 0, 0, 0))
        else:
            kv_block = pl.BlockSpec((seqs, None, kv_seq_len, N_KV_HEADS, HEAD_DIM), lambda i: (i, l, 0, 0, 0))
        out_specs += [kv_block] * 2
        if kv_cache is not None:
            aliases = {len(args): 6, len(args) + 1: 7}
            in_specs += [pl.BlockSpec(memory_space=pl.ANY)] * 2
            args += list(kv_cache)
    return pl.pallas_call(
        functools.partial(_inproj_kernel, rope=rope, keep_kv=keep_kv, row_split=ROW_SPLIT,
                          seq_len=kv_seq_len, slot=l if creates else None),
        grid=(n // tm,),
        in_specs=in_specs,
        out_specs=out_specs,
        out_shape=out_shape,
        input_output_aliases=aliases,
        compiler_params=_params(),
        name="inproj_rope" if rope else "inproj",
    )(*args)


def _stack_heads(q_at, g):
    return jnp.concatenate(
        [q_at(slice((g * Q_PER_KV + j) * HEAD_DIM, (g * Q_PER_KV + j + 1) * HEAD_DIM))
         for j in range(Q_PER_KV)], axis=0)


def _sink_softmax_rows(s_scr, p_scr, den_scr, m_scr, sink_ref, head0, rows_per_head, rch, n_heads=Q_PER_KV):
    c1 = SCALE * LOG2E
    cols = [slice(c * LANES, (c + 1) * LANES) for c in range(s_scr.shape[1] // LANES)]
    chunks = [(pl.ds(ci * rch, rch), head0 + ci * rch // rows_per_head)
              for ci in range(n_heads * rows_per_head // rch)]
    for rws, head in chunks:
        mx = s_scr[rws, cols[0]]
        for c in cols[1:]:
            mx = jnp.maximum(mx, s_scr[rws, c])
        row_max = jnp.broadcast_to(jnp.max(mx, axis=-1, keepdims=True), mx.shape)
        m_scr[rws, :] = jnp.maximum(row_max * SCALE, sink_ref[head])
    for rws, head in chunks:
        m = m_scr[rws, :]
        mr = m * (1.0 / SCALE)
        acc = None
        for c in cols:
            e = jnp.exp2((s_scr[rws, c] - mr) * c1)
            p_scr[rws, c] = e.astype(BF16)
            acc = e if acc is None else acc + e
        row_sum = jnp.broadcast_to(jnp.sum(acc, axis=-1, keepdims=True), acc.shape)
        den_scr[rws, :] = row_sum + jnp.exp(sink_ref[head] - m)


def _ctx_attn_kernel(sink_ref, q_ref, k_ref, v_ref, o_ref, s_scr, p_scr, den_scr, m_scr):
    bt, t = q_ref.shape[0], q_ref.shape[1]
    units = [(b, g) for b in range(bt) for g in range(N_KV_HEADS)]

    def scores(n, b, g):
        q4 = _stack_heads(lambda s: q_ref[b, :, s], g)
        s_scr[n % 2] = _dot_nt(q4, k_ref[b, :, g * HEAD_DIM:(g + 1) * HEAD_DIM])

    scores(0, *units[0])
    for n, (b, g) in enumerate(units):
        if n + 1 < len(units):
            scores(n + 1, *units[n + 1])
        c = n % 2
        _sink_softmax_rows(s_scr.at[c], p_scr.at[c], den_scr.at[c], m_scr.at[c], sink_ref, g * Q_PER_KV, t, ATTN_ROWS)
        o = _dot(p_scr[c], v_ref[b, :, g * HEAD_DIM:(g + 1) * HEAD_DIM]) / den_scr[c]
        for j in range(Q_PER_KV):
            hs = slice((g * Q_PER_KV + j) * HEAD_DIM, (g * Q_PER_KV + j + 1) * HEAD_DIM)
            o_ref[b, :, hs] = o[j * t:(j + 1) * t].astype(BF16)


def _ctx_attn(sink, q, k, v):
    b, t = q.shape[0], q.shape[1]
    bt = 4
    blk = lambda w: pl.BlockSpec((bt, t, w), lambda i: (i, 0, 0))
    return pl.pallas_call(
        _ctx_attn_kernel,
        grid=(b // bt,),
        in_specs=[pl.BlockSpec(memory_space=pltpu.SMEM), blk(ATTN_WIDTH), blk(KV_WIDTH), blk(KV_WIDTH)],
        out_specs=blk(ATTN_WIDTH),
        out_shape=jax.ShapeDtypeStruct((b, t, ATTN_WIDTH), BF16),
        scratch_shapes=[pltpu.VMEM((N_KV_HEADS, Q_PER_KV * t, t), F32),
                        pltpu.VMEM((N_KV_HEADS, Q_PER_KV * t, t), BF16),
                        pltpu.VMEM((N_KV_HEADS, Q_PER_KV * t, LANES), F32),
                        pltpu.VMEM((N_KV_HEADS, Q_PER_KV * t, LANES), F32)],
        compiler_params=_params(),
        name="ctx_attn",
    )(sink, q, k, v)


def _band_attn_kernel(sink_ref, q_ref, kp_ref, kc_ref, kn_ref, vp_ref, vc_ref, vn_ref,
                      ck_ref, cv_ref, o_ref, s_scr, p_scr, den_scr, m_scr):
    i = pl.program_id(1)
    nb = pl.num_programs(1)
    past = ck_ref.shape[0]
    rows = Q_PER_KV * BLOCK

    qrow = lax.broadcasted_iota(jnp.int32, (rows, BLOCK), 0) & (BLOCK - 1)
    col = lax.broadcasted_iota(jnp.int32, (rows, BLOCK), 1)
    in_prev = col >= qrow + jnp.where(i > 0, 0, 2 * BLOCK)
    in_next = col <= qrow - jnp.where(i < nb - 1, 0, 2 * BLOCK)

    def scores(g):
        sl = slice(g * HEAD_DIM, (g + 1) * HEAD_DIM)
        q4 = _stack_heads(lambda s: q_ref[:, s], g)
        kloc = jnp.concatenate([kp_ref[:, sl], kc_ref[:, sl], kn_ref[:, sl]], axis=0)
        s_scr[g, :, 0:past] = _dot_nt(q4, ck_ref[:, sl])
        s_loc = _dot_nt(q4, kloc)
        s_scr[g, :, past:past + BLOCK] = jnp.where(in_prev, s_loc[:, 0:BLOCK], NEG)
        s_scr[g, :, past + BLOCK:past + 2 * BLOCK] = s_loc[:, BLOCK:2 * BLOCK]
        s_scr[g, :, past + 2 * BLOCK:] = jnp.where(in_next, s_loc[:, 2 * BLOCK:], NEG)

    def values(g):
        sl = slice(g * HEAD_DIM, (g + 1) * HEAD_DIM)
        vloc = jnp.concatenate([vp_ref[:, sl], vc_ref[:, sl], vn_ref[:, sl]], axis=0)
        o = (_dot(p_scr[g, :, 0:past], cv_ref[:, sl]) + _dot(p_scr[g, :, past:], vloc)) / den_scr[g]
        for j in range(Q_PER_KV):
            hs = slice((g * Q_PER_KV + j) * HEAD_DIM, (g * Q_PER_KV + j + 1) * HEAD_DIM)
            o_ref[:, hs] = o[j * BLOCK:(j + 1) * BLOCK].astype(BF16)

    for g in range(N_KV_HEADS):
        scores(g)
    for g in range(N_KV_HEADS):
        _sink_softmax_rows(s_scr.at[g], p_scr.at[g], den_scr.at[g], m_scr.at[g], sink_ref, g * Q_PER_KV, BLOCK, ATTN_ROWS)
        values(g)


def _band_attn(sink, q, k, v, ctx_k, ctx_v, l):
    b, t = q.shape[0], q.shape[1]
    nb = t // BLOCK
    past = ctx_k.shape[2]
    keys = past + 3 * BLOCK
    cur = lambda w: pl.BlockSpec((None, BLOCK, w), lambda bi, i: (bi, i, 0))
    prev = pl.BlockSpec((None, BLOCK, KV_WIDTH), lambda bi, i: (bi, jnp.maximum(i - 1, 0), 0))
    nxt = pl.BlockSpec((None, BLOCK, KV_WIDTH), lambda bi, i: (bi, jnp.minimum(i + 1, nb - 1), 0))
    ctx = pl.BlockSpec((None, None, past, KV_WIDTH), lambda bi, i: (bi, l, 0, 0))
    return pl.pallas_call(
        _band_attn_kernel,
        grid=(b, nb),
        in_specs=[pl.BlockSpec(memory_space=pltpu.SMEM), cur(ATTN_WIDTH),
                  prev, cur(KV_WIDTH), nxt, prev, cur(KV_WIDTH), nxt, ctx, ctx],
        out_specs=cur(ATTN_WIDTH),
        out_shape=jax.ShapeDtypeStruct((b, t, ATTN_WIDTH), BF16),
        scratch_shapes=[pltpu.VMEM((N_KV_HEADS, Q_PER_KV * BLOCK, keys), F32),
                        pltpu.VMEM((N_KV_HEADS, Q_PER_KV * BLOCK, keys), BF16),
                        pltpu.VMEM((N_KV_HEADS, Q_PER_KV * BLOCK, LANES), F32),
                        pltpu.VMEM((N_KV_HEADS, Q_PER_KV * BLOCK, LANES), F32)],
        compiler_params=_params(),
        name="band_attn",
    )(sink, q, k, k, k, v, v, v, ctx_k, ctx_v)


def _lru_kernel(xr_ref, gx_ref, cw_ref, cb_ref, lam_ref, wg_ref, bg_ref, h0_ref,
                y_ref, st_ref, xpad, ldec, af, bf, ab, bb, *, seq_len, seg, groups):
    t = seq_len
    n_slab = xr_ref.shape[1] // LANES
    rc = min(LRU_CHUNK, t)
    n_chunks = t // rc
    n_seg = SUBLANES * groups
    gate_slabs = min(GATE_SLABS, n_slab)
    rows = n_seg * seg

    for s in range(n_slab):
        xpad[s, 0:SUBLANES, :] = jnp.zeros((SUBLANES, LANES), F32)
        xpad[s, SUBLANES + t:2 * SUBLANES + t, :] = jnp.zeros((SUBLANES, LANES), F32)
        xpad[s, SUBLANES:SUBLANES + t, :] = xr_ref[:, s * LANES:(s + 1) * LANES]
        af[s, t:rows, :] = jnp.ones((rows - t, LANES), F32)
        ab[s, t:rows, :] = jnp.ones((rows - t, LANES), F32)
        bf[s, t:rows, :] = jnp.zeros((rows - t, LANES), F32)
        bb[s, t:rows, :] = jnp.zeros((rows - t, LANES), F32)

    ldec[...] = (-0.5 * LRU_C) * _softplus(-lam_ref[...])

    def gates_chunk(idx, carry):
        sp = idx // n_chunks
        r0 = pl.multiple_of((idx - sp * n_chunks) * rc, rc)
        base = r0 + SUBLANES - CONV_LEFT
        for k in range(gate_slabs):
            s = sp * gate_slabs + k
            cw = cw_ref[s]
            xc = cw[0:1, :] * xpad[s, pl.ds(base, rc, stride=1), :]
            for tap in range(1, CONV_W):
                xc = xc + cw[tap:tap + 1, :] * xpad[s, pl.ds(base + tap, rc, stride=1), :]
            xc = xc + cb_ref[s]
            half_x = 0.5 * xc
            gh = _dot(half_x.astype(BF16), wg_ref[s]) + 0.5 * bg_ref[s]
            hd = ldec[s]
            for d, (a_scr, b_scr) in enumerate(((af, bf), (ab, bb))):
                off = 2 * d * LANES
                th_r = jnp.tanh(gh[:, off:off + LANES])
                th_i = jnp.tanh(gh[:, off + LANES:off + 2 * LANES])
                log_a = hd[d:d + 1, :] * th_r + hd[d:d + 1, :]
                a = jnp.exp(log_a)
                u = jnp.tanh(log_a) * (-1.0 - a * a)
                root = jnp.where(u == 0.0, 0.0, u * lax.rsqrt(u))
                a_scr[s, pl.ds(r0, rc), :] = a
                b_scr[s, pl.ds(r0, rc), :] = root * (half_x * th_i + half_x)
        return carry

    lax.fori_loop(0, (n_slab // gate_slabs) * n_chunks, gates_chunk, 0)

    def seg_rows(g, i):
        return pl.ds(g * SUBLANES * seg + i, SUBLANES, stride=seg)

    ones = jnp.ones((SUBLANES, LANES), F32)
    zeros = jnp.zeros((SUBLANES, LANES), F32)
    slabs_per_pass = max(1, SCAN_CHAINS // groups)
    chains = [(k, g) for k in range(slabs_per_pass) for g in range(groups)]

    def scan_slabs(sg, carry):
        s0 = sg * slabs_per_pass

        totals = [(ones, zeros, ones, zeros) for _ in chains]
        for i in range(seg):
            ib = seg - 1 - i
            for n, (k, g) in enumerate(chains):
                pf, hf, pb, hb = totals[n]
                a = af[s0 + k, seg_rows(g, i), :]
                a2 = ab[s0 + k, seg_rows(g, ib), :]
                totals[n] = (pf * a, a * hf + bf[s0 + k, seg_rows(g, i), :],
                             pb * a2, a2 * hb + bb[s0 + k, seg_rows(g, ib), :])

        starts = {}
        for k in range(slabs_per_pass):
            c = h0_ref[s0 + k, 0:1, :]
            for g in range(groups):
                pf, hf, _, _ = totals[k * groups + g]
                fwd = []
                for j in range(SUBLANES):
                    fwd.append(c)
                    c = hf[j:j + 1] + pf[j:j + 1] * c
                starts[(k, g, 0)] = jnp.concatenate(fwd, axis=0)
            st_ref[s0 + k, 0:1, :] = c
            c = h0_ref[s0 + k, 1:2, :]
            for g in reversed(range(groups)):
                _, _, pb, hb = totals[k * groups + g]
                bwd = [None] * SUBLANES
                for j in reversed(range(SUBLANES)):
                    bwd[j] = c
                    c = hb[j:j + 1] + pb[j:j + 1] * c
                starts[(k, g, 1)] = jnp.concatenate(bwd, axis=0)
            st_ref[s0 + k, 1:2, :] = c

        state = [(starts[(k, g, 0)], starts[(k, g, 1)]) for (k, g) in chains]
        for i in range(seg):
            ib = seg - 1 - i
            for n, (k, g) in enumerate(chains):
                hf, hb = state[n]
                hf = af[s0 + k, seg_rows(g, i), :] * hf + bf[s0 + k, seg_rows(g, i), :]
                hb = ab[s0 + k, seg_rows(g, ib), :] * hb + bb[s0 + k, seg_rows(g, ib), :]
                bf[s0 + k, seg_rows(g, i), :] = hf
                bb[s0 + k, seg_rows(g, ib), :] = hb
                state[n] = (hf, hb)
        return carry

    lax.fori_loop(0, n_slab // slabs_per_pass, scan_slabs, 0)

    def out_chunk(ci, carry):
        r0 = pl.multiple_of(ci * rc, rc)
        for s in range(n_slab):
            sl = slice(s * LANES, (s + 1) * LANES)
            hsum = bf[s, pl.ds(r0, rc), :] + bb[s, pl.ds(r0, rc), :]
            y_ref[pl.ds(r0, rc), sl] = (hsum * gx_ref[pl.ds(r0, rc), sl]).astype(BF16)
        return carry

    lax.fori_loop(0, n_chunks, out_chunk, 0)


def _lru(xr, gx, h0, l, seq_len, conv_w, conv_b, lam, wg, bg, cb, groups):
    n = xr.shape[0]
    b = n // seq_len
    n_slab = cb // LANES
    seg = seq_len // (SUBLANES * groups) + SEG_PAD
    assert n_slab % min(GATE_SLABS, n_slab) == 0 and n_slab % max(1, SCAN_CHAINS // groups) == 0
    rows = SUBLANES * groups * seg
    col = lambda bi, j: (bi, j)
    par = lambda r, w: pl.BlockSpec((None, n_slab, r, w), lambda bi, j: (l, j, 0, 0))
    state = pl.BlockSpec((None, n_slab, 2, LANES), lambda bi, j: (bi, j, 0, 0))
    scan_buf = pltpu.VMEM((n_slab, rows, LANES), F32)
    return pl.pallas_call(
        functools.partial(_lru_kernel, seq_len=seq_len, seg=seg, groups=groups),
        grid=(b, D_RNN // cb),
        in_specs=[
            pl.BlockSpec((seq_len, cb), col),
            pl.BlockSpec((seq_len, cb), col),
            par(CONV_W, LANES), par(1, LANES), par(2, LANES),
            par(RNN_BLOCK_W, 4 * RNN_BLOCK_W), par(1, 4 * RNN_BLOCK_W),
            state,
        ],
        out_specs=[pl.BlockSpec((seq_len, cb), col), state],
        out_shape=[
            jax.ShapeDtypeStruct((n, D_RNN), BF16),
            jax.ShapeDtypeStruct((b, N_RNN_BLOCKS, 2, LANES), F32),
        ],
        scratch_shapes=[pltpu.VMEM((n_slab, seq_len + 2 * SUBLANES, LANES), F32),
                        pltpu.VMEM((n_slab, 2, LANES), F32),
                        scan_buf, scan_buf, scan_buf, scan_buf],
        compiler_params=_params(),
        name="rglru",
    )(xr, gx, conv_w, conv_b, lam, wg, bg, h0)


def _post_kernel(x_ref, attn_ref, lru_ref, g_ref, mod_ref, nw_ref, wa_ref, wl_ref, wo_ref,
                 w1_ref, w2_ref, *rest):
    n_cast = (len(rest) - 1) // 2
    o_ref = rest[n_cast]
    for src, dst in zip(rest[:n_cast], rest[n_cast + 1:]):
        dst[...] = src[...].astype(BF16)
    half = x_ref.shape[0] // ROW_SPLIT
    for r in range(ROW_SPLIT):
        rows = slice(r * half, (r + 1) * half)
        a = _dot(attn_ref[rows, :], wa_ref[...])
        b = _dot(lru_ref[rows, :], wl_ref[...])
        merged = _sigmoid(g_ref[rows, 0:D_MODEL]) * a + _sigmoid(g_ref[rows, D_MODEL:]) * b
        x = x_ref[rows, :] + mod_ref[2:3, :] * _dot(merged.astype(BF16), wo_ref[...])
        h = (_rms(x) * nw_ref[...] * (1.0 + mod_ref[4:5, :]) + mod_ref[3:4, :]).astype(BF16)
        ff = D_MODEL
        acc = None
        for c in range(D_FF // ff):
            hid = jnp.maximum(_dot(h, w1_ref[:, c * ff:(c + 1) * ff]), 0.0)
            part = _dot((hid * hid).astype(BF16), w2_ref[c * ff:(c + 1) * ff, :])
            acc = part if acc is None else acc + part
        o_ref[rows, :] = x + mod_ref[5:6, :] * acc


def _post(x, attn, lru, gates, mod, l, cond_of_tile, norm2_w, w_attn_o, w_lru_o, w_out, w1, w2, in_place, cast=()):
    n = x.shape[0]
    tm = ROW_TILE
    steps = n // tm
    row = lambda i: (i, 0)
    cast_in = [pl.BlockSpec((None, w.shape[1] // steps, w.shape[2]), lambda i: (l + 1, i, 0)) for w in cast]
    cast_out = [pl.BlockSpec((w.shape[1] // steps, w.shape[2]), row) for w in cast]
    outs = pl.pallas_call(
        _post_kernel,
        grid=(steps,),
        in_specs=[
            pl.BlockSpec((tm, D_MODEL), row),
            pl.BlockSpec((tm, ATTN_WIDTH), row),
            pl.BlockSpec((tm, D_RNN), row),
            pl.BlockSpec((tm, 2 * D_MODEL), row),
            pl.BlockSpec((None, None, N_MOD, D_MODEL), lambda i: (l, cond_of_tile(i), 0, 0)),
            _resident((None, 1, D_MODEL), lambda i: (l, 0, 0)),
            _layer_weight(w_attn_o, l), _layer_weight(w_lru_o, l), _layer_weight(w_out, l),
            _layer_weight(w1, l), _layer_weight(w2, l),
        ] + cast_in,
        out_specs=[pl.BlockSpec((tm, D_MODEL), row)] + cast_out,
        out_shape=[jax.ShapeDtypeStruct((n, D_MODEL), F32)]
        + [jax.ShapeDtypeStruct(w.shape[1:], BF16) for w in cast],
        input_output_aliases={0: 0} if in_place else {},
        compiler_params=_params(),
        name="post",
    )(x, attn, lru, gates, mod, norm2_w, w_attn_o, w_lru_o, w_out, w1, w2, *cast)
    return outs[0], outs[1:]


def _rope_tables(n_tokens):
    rows = n_tokens // GRID_W
    row = jnp.broadcast_to(jnp.arange(rows)[:, None], (rows, GRID_W)).reshape(-1).astype(F32)
    col = jnp.broadcast_to(jnp.arange(GRID_W)[None, :], (rows, GRID_W)).reshape(-1).astype(F32)
    freqs = ROPE_THETA ** (-jnp.arange(0, ROPE_AXIS_DIM, 2, dtype=F32) / ROPE_AXIS_DIM)
    ang_r = row[:, None] * freqs
    ang_c = col[:, None] * freqs
    cr, sr, cc, sc = jnp.cos(ang_r), jnp.sin(ang_r), jnp.cos(ang_c), jnp.sin(ang_c)
    return (jnp.concatenate([cr, cr, cc, cc], axis=-1),
            jnp.concatenate([-sr, sr, -sc, sc], axis=-1))


def kernel(x_prompt, x_sample, cache_k, cache_v, state_lru, c, c_ctx, ada_w, ada_b, norm1_w, norm2_w, w_in, q_norm_w, k_norm_w, attn_sink, conv_w, conv_b, lru_lambda, lru_wa, lru_ba, lru_wi, lru_bi, w_attn_o, w_lru_o, w_out, mlp_w1, mlp_w2):
    bp, tp, _ = x_prompt.shape
    bs, ts, _ = x_sample.shape
    past = cache_k.shape[2]
    assert bs + 1 <= N_COND and ts % ROW_TILE == 0 and (bp * tp) % ROW_TILE == 0

    cond = jnp.zeros((N_COND, D_MODEL), F32).at[0].set(c_ctx).at[1:1 + bs].set(c)
    mod = _adaln(cond, ada_w, ada_b).reshape(DEPTH, N_COND, N_MOD, D_MODEL)
    tiles_per_sample = ts // ROW_TILE
    cond_prompt = lambda i: 0
    cond_sample = lambda i: 1 + i // tiles_per_sample

    w_in_b = w_in[0].astype(BF16)
    post_w = [w[0].astype(BF16) for w in (w_attn_o, w_lru_o, w_out, mlp_w1, mlp_w2)]
    wg = jnp.concatenate([lru_wa[:, 0], lru_wi[:, 0], lru_wa[:, 1], lru_wi[:, 1]], axis=-1).astype(BF16)
    blk = lambda v: v.reshape(DEPTH, N_RNN_BLOCKS, 1, RNN_BLOCK_W)
    bg = jnp.concatenate([blk(lru_ba[:, 0]), blk(lru_bi[:, 0]), blk(lru_ba[:, 1]), blk(lru_bi[:, 1])], axis=-1)
    ctx_k = cache_k.reshape(bs, DEPTH, past, KV_WIDTH).astype(BF16)
    ctx_v = cache_v.reshape(bs, DEPTH, past, KV_WIDTH).astype(BF16)
    rope_tabs = _rope_tables(ts)

    n1 = norm1_w.reshape(DEPTH, 1, D_MODEL)
    n2 = norm2_w.reshape(DEPTH, 1, D_MODEL)
    qw = q_norm_w.reshape(DEPTH, 1, HEAD_DIM)
    kw = k_norm_w.reshape(DEPTH, 1, HEAD_DIM)
    by_block = lambda v: jnp.swapaxes(v.reshape(DEPTH, -1, N_RNN_BLOCKS, RNN_BLOCK_W), 1, 2)
    cvw = by_block(conv_w)
    cvb = by_block(conv_b)
    lam = by_block(lru_lambda)
    state_by_block = lambda v: jnp.swapaxes(v.reshape(-1, 2, N_RNN_BLOCKS, RNN_BLOCK_W), 1, 2)
    zero_state = jnp.zeros((bp, N_RNN_BLOCKS, 2, RNN_BLOCK_W), F32)

    xp = x_prompt.reshape(bp * tp, D_MODEL)
    xs = x_sample.reshape(bs * ts, D_MODEL)
    kv_cache, new_s = None, []
    for l in range(DEPTH):
        in_place = l > 0
        q, k, v, xr, gx, gates, *kv_cache = _inproj(xp, mod, l, cond_prompt, n1, w_in_b, qw, kw, kv_cache=kv_cache,
                                                    kv_seq_len=tp, tm=ROW_TILE if kv_cache else tp)
        attn = _ctx_attn(attn_sink[l], q.reshape(bp, tp, -1), k.reshape(bp, tp, -1), v.reshape(bp, tp, -1))
        lru, st = _lru(xr, gx, zero_state, l, tp, cvw, cvb, lam, wg, bg, D_RNN, 1)
        more = l + 1 < DEPTH
        xp, next_w_in = _post(xp, attn.reshape(bp * tp, -1), lru, gates, mod, l, cond_prompt, n2,
                              *post_w, in_place, cast=(w_in,) if more else ())
        new_s.append(jnp.swapaxes(st, 1, 2).reshape(bp, 2, D_RNN))
        q, k, v, xr, gx, gates = _inproj(xs, mod, l, cond_sample, n1, w_in_b, qw, kw, rope_tabs=rope_tabs)
        attn = _band_attn(attn_sink[l], q.reshape(bs, ts, -1), k.reshape(bs, ts, -1),
                          v.reshape(bs, ts, -1), ctx_k, ctx_v, l)
        lru, _ = _lru(xr, gx, state_by_block(state_lru[:, l]), l, ts, cvw, cvb, lam, wg, bg, 512, 2)
        xs, next_post_w = _post(xs, attn.reshape(bs * ts, -1), lru, gates, mod, l, cond_sample, n2,
                                *post_w, in_place,
                                cast=(w_attn_o, w_lru_o, w_out, mlp_w1, mlp_w2) if more else ())
        if more:
            (w_in_b,), post_w = next_w_in, next_post_w

    new_k_arr, new_v_arr = kv_cache
    return (xp.reshape(bp, tp, D_MODEL), xs.reshape(bs, ts, D_MODEL), new_k_arr, new_v_arr,
            jnp.stack(new_s, axis=1))
```

```python
import functools

import jax
import jax.numpy as jnp
from jax import lax
from jax.experimental import pallas as pl
from jax.experimental.pallas import tpu as pltpu

D_MODEL = 1024
DEPTH = 4
GRID_W = 64
HEAD_DIM = 128
N_Q_HEADS = 8
N_KV_HEADS = 2
Q_PER_KV = N_Q_HEADS // N_KV_HEADS
ATTN_WIDTH = N_Q_HEADS * HEAD_DIM
KV_WIDTH = N_KV_HEADS * HEAD_DIM
WINDOW = 128
BLOCK = 128
SCALE = HEAD_DIM ** -0.5
ROPE_AXIS_DIM = HEAD_DIM // 2
ROPE_THETA = 10000.0
D_RNN = D_MODEL
N_RNN_BLOCKS = 8
RNN_BLOCK_W = D_RNN // N_RNN_BLOCKS
CONV_W = 4
CONV_LEFT = 2
LRU_C = 8.0
D_FF = 4 * D_MODEL
N_MOD = 6
EPS = 1e-6
NEG = -1e30
SPLIT_Q = ATTN_WIDTH
SPLIT_K = SPLIT_Q + KV_WIDTH
SPLIT_V = SPLIT_K + KV_WIDTH
SPLIT_XR = SPLIT_V + D_RNN
SPLIT_XG = SPLIT_XR + D_RNN
IN_WIDTH = SPLIT_XG + 2 * D_MODEL

LANES = 128
SUBLANES = 8
VMEM_LIMIT = 56 * 1024 * 1024

ROW_TILE = 512
ROW_SPLIT = 2
N_COND = 8
SEG_PAD = 4
ATTN_ROWS = 64
LOG2E = 1.4426950408889634
LRU_CHUNK = 256
SCAN_CHAINS = 4
GATE_SLABS = 4

F32 = jnp.float32
BF16 = jnp.bfloat16


def _sigmoid(x):
    return 0.5 * jnp.tanh(0.5 * x) + 0.5


def _gelu_tanh(x):
    c = 0.7978845608028654
    return 0.5 * x * (1.0 + jnp.tanh(c * (x + 0.044715 * (x * x * x))))


def _softplus(x):
    return jnp.maximum(x, 0.0) + jnp.log1p(jnp.exp(-jnp.abs(x)))


def _dot(a, b):
    return jnp.dot(a, b, preferred_element_type=F32)


def _dot_nt(a, b):
    return lax.dot_general(a, b, (((1,), (1,)), ((), ())), preferred_element_type=F32)


def _rms(x):
    return x * lax.rsqrt(jnp.mean(x * x, axis=-1, keepdims=True) + EPS)


def _resident(shape, index_map):
    return pl.BlockSpec(shape, index_map, pipeline_mode=pl.Buffered(1))


def _layer_weight(w, l):
    if w.ndim == 3:
        return _resident((None,) + w.shape[1:], lambda i: (l, 0, 0))
    return _resident(w.shape, lambda i: (0, 0))


def _params():
    return pltpu.CompilerParams(vmem_limit_bytes=VMEM_LIMIT)


def _adaln_kernel(cond_ref, w_ref, b_ref, o_ref):
    cnd = cond_ref[...]
    act = (cnd * _sigmoid(cnd)).astype(BF16)
    o_ref[...] = _dot(act, w_ref[...].astype(BF16)) + b_ref[...]


def _adaln(cond, ada_w, ada_b):
    tn = 1536
    width = N_MOD * D_MODEL
    return pl.pallas_call(
        _adaln_kernel,
        grid=(DEPTH, width // tn),
        in_specs=[
            pl.BlockSpec((N_COND, D_MODEL), lambda l, j: (0, 0)),
            pl.BlockSpec((None, D_MODEL, tn), lambda l, j: (l, 0, j)),
            pl.BlockSpec((None, 1, tn), lambda l, j: (l, 0, j)),
        ],
        out_specs=pl.BlockSpec((None, N_COND, tn), lambda l, j: (l, 0, j)),
        out_shape=jax.ShapeDtypeStruct((DEPTH, N_COND, width), F32),
        compiler_params=_params(),
        name="adaln",
    )(cond, ada_w, ada_b.reshape(DEPTH, 1, width))


def _rope(x, cs, sn):
    lane = lax.broadcasted_iota(jnp.int32, x.shape, 1)
    half = ROPE_AXIS_DIM // 2
    swapped = jnp.where((lane & (ROPE_AXIS_DIM - 1)) < half,
                        pltpu.roll(x, HEAD_DIM - half, 1), pltpu.roll(x, half, 1))
    return x * cs + swapped * sn


def _inproj_kernel(*refs, rope, keep_kv, row_split, seq_len, slot):
    x_ref, mod_ref, nw_ref, w_ref, qw_ref, kw_ref = refs[:6]
    refs = refs[6:]
    if rope:
        cs_ref, sn_ref = refs[:2]
        refs = refs[2:]
    if keep_kv:
        refs = refs[-8:]
        k32_ref, v32_ref = refs[6:8]
    q_ref, k_ref, v_ref, xr_ref, gx_ref, g_ref = refs[:6]

    half = x_ref.shape[0] // row_split
    hs = []
    for r in range(row_split):
        rows = slice(r * half, (r + 1) * half)
        y = _rms(x_ref[rows, :]) * nw_ref[...]
        hs.append((y * (1.0 + mod_ref[1:2, :]) + mod_ref[0:1, :]).astype(BF16))
    if keep_kv and slot is not None:
        for ref in (k32_ref, v32_ref):
            for other in range(DEPTH):
                if other != slot:
                    ref[:, other] = jnp.zeros((ref.shape[0],) + ref.shape[2:], F32)
    for r in range(row_split):
        rows = slice(r * half, (r + 1) * half)
        h = hs[r]
        if keep_kv:
            kv_at = (r * half // seq_len,) + (() if slot is None else (slot,)) + (pl.ds(r * half % seq_len, half),)

        def head(z, w):
            o = _rms(z) * w
            return _rope(o, cs_ref[rows, :], sn_ref[rows, :]) if rope else o

        zq = _dot(h, w_ref[:, 0:SPLIT_Q])
        for hd in range(N_Q_HEADS):
            sl = slice(hd * HEAD_DIM, (hd + 1) * HEAD_DIM)
            q_ref[rows, sl] = head(zq[:, sl], qw_ref[...]).astype(BF16)

        zkv = _dot(h, w_ref[:, SPLIT_Q:SPLIT_V])
        for hd in range(N_KV_HEADS):
            sl = slice(hd * HEAD_DIM, (hd + 1) * HEAD_DIM)
            kn = head(zkv[:, sl], kw_ref[...])
            k_ref[rows, sl] = kn.astype(BF16)
            if keep_kv:
                k32_ref[kv_at + (hd, slice(None))] = kn
        zv = zkv[:, KV_WIDTH:2 * KV_WIDTH]
        v_ref[rows, :] = zv.astype(BF16)
        if keep_kv:
            for hd in range(N_KV_HEADS):
                v32_ref[kv_at + (hd, slice(None))] = zv[:, hd * HEAD_DIM:(hd + 1) * HEAD_DIM]

        xr_ref[rows, :] = _dot(h, w_ref[:, SPLIT_V:SPLIT_XR])
        gx_ref[rows, :] = _gelu_tanh(_dot(h, w_ref[:, SPLIT_XR:SPLIT_XG]))
        g_ref[rows, 0:D_MODEL] = _dot(h, w_ref[:, SPLIT_XG:SPLIT_XG + D_MODEL])
        g_ref[rows, D_MODEL:] = _dot(h, w_ref[:, SPLIT_XG + D_MODEL:])


def _inproj(x, mod, l, cond_of_tile, norm1_w, w_in, q_norm_w, k_norm_w, rope_tabs=None, kv_cache=None,
            kv_seq_len=None, tm=ROW_TILE):
    n = x.shape[0]
    rope = rope_tabs is not None
    keep_kv = kv_seq_len is not None
    creates = keep_kv and kv_cache is None
    row = lambda i: (i, 0)
    in_specs = [
        pl.BlockSpec((tm, D_MODEL), row),
        pl.BlockSpec((None, None, N_MOD, D_MODEL), lambda i: (l, cond_of_tile(i), 0, 0)),
        _resident((None, 1, D_MODEL), lambda i: (l, 0, 0)),
        _layer_weight(w_in, l),
        _resident((None, 1, HEAD_DIM), lambda i: (l, 0, 0)),
        _resident((None, 1, HEAD_DIM), lambda i: (l, 0, 0)),
    ]
    args = [x, mod, norm1_w, w_in, q_norm_w, k_norm_w]
    if rope:
        tiles_per_seq = rope_tabs[0].shape[0] // tm
        tab = pl.BlockSpec((tm, HEAD_DIM), lambda i: (i % tiles_per_seq, 0))
        in_specs += [tab, tab]
        args += list(rope_tabs)
    out_shape = [
        jax.ShapeDtypeStruct((n, ATTN_WIDTH), BF16),
        jax.ShapeDtypeStruct((n, KV_WIDTH), BF16),
        jax.ShapeDtypeStruct((n, KV_WIDTH), BF16),
        jax.ShapeDtypeStruct((n, D_RNN), F32),
        jax.ShapeDtypeStruct((n, D_RNN), F32),
        jax.ShapeDtypeStruct((n, 2 * D_MODEL), F32),
    ]
    out_specs = [
        pl.BlockSpec((tm, ATTN_WIDTH), row),
        pl.BlockSpec((tm, KV_WIDTH), row),
        pl.BlockSpec((tm, KV_WIDTH), row),
        pl.BlockSpec((tm, D_RNN), row),
        pl.BlockSpec((tm, D_RNN), row),
        pl.BlockSpec((tm, 2 * D_MODEL), row),
    ]
    aliases = {}
    if keep_kv:
        seqs = tm // kv_seq_len
        assert tm % kv_seq_len == 0 and n % kv_seq_len == 0
        kv_shape = (n // kv_seq_len, DEPTH, kv_seq_len, N_KV_HEADS, HEAD_DIM)
        out_shape += [jax.ShapeDtypeStruct(kv_shape, F32)] * 2
        if creates:
            kv_block = pl.BlockSpec((seqs, DEPTH, kv_seq_len, N_KV_HEADS, HEAD_DIM), lambda i: (i, 0, 0, 0, 0))
        else:
            kv_block = pl.BlockSpec((seqs, None, kv_seq_len, N_KV_HEADS, HEAD_DIM), lambda i: (i, l, 0, 0, 0))
        out_specs += [kv_block] * 2
        if kv_cache is not None:
            aliases = {len(args): 6, len(args) + 1: 7}
            in_specs += [pl.BlockSpec(memory_space=pl.ANY)] * 2
            args += list(kv_cache)
    return pl.pallas_call(
        functools.partial(_inproj_kernel, rope=rope, keep_kv=keep_kv, row_split=ROW_SPLIT,
                          seq_len=kv_seq_len, slot=l if creates else None),
        grid=(n // tm,),
        in_specs=in_specs,
        out_specs=out_specs,
        out_shape=out_shape,
        input_output_aliases=aliases,
        compiler_params=_params(),
        name="inproj_rope" if rope else "inproj",
    )(*args)


def _stack_heads(q_at, g):
    return jnp.concatenate(
        [q_at(slice((g * Q_PER_KV + j) * HEAD_DIM, (g * Q_PER_KV + j + 1) * HEAD_DIM))
         for j in range(Q_PER_KV)], axis=0)


def _sink_softmax_rows(s_scr, p_scr, den_scr, m_scr, sink_ref, head0, rows_per_head, rch, n_heads=Q_PER_KV):
    c1 = SCALE * LOG2E
    cols = [slice(c * LANES, (c + 1) * LANES) for c in range(s_scr.shape[1] // LANES)]
    chunks = [(pl.ds(ci * rch, rch), head0 + ci * rch // rows_per_head)
              for ci in range(n_heads * rows_per_head // rch)]
    for rws, head in chunks:
        mx = s_scr[rws, cols[0]]
        for c in cols[1:]:
            mx = jnp.maximum(mx, s_scr[rws, c])
        row_max = jnp.broadcast_to(jnp.max(mx, axis=-1, keepdims=True), mx.shape)
        m_scr[rws, :] = jnp.maximum(row_max * SCALE, sink_ref[head])
    for rws, head in chunks:
        m = m_scr[rws, :]
        mr = m * (1.0 / SCALE)
        acc = None
        for c in cols:
            e = jnp.exp2((s_scr[rws, c] - mr) * c1)
            p_scr[rws, c] = e.astype(BF16)
            acc = e if acc is None else acc + e
        row_sum = jnp.broadcast_to(jnp.sum(acc, axis=-1, keepdims=True), acc.shape)
        den_scr[rws, :] = row_sum + jnp.exp(sink_ref[head] - m)


def _ctx_attn_kernel(sink_ref, q_ref, k_ref, v_ref, o_ref, s_scr, p_scr, den_scr, m_scr):
    bt, t = q_ref.shape[0], q_ref.shape[1]
    units = [(b, g) for b in range(bt) for g in range(N_KV_HEADS)]

    def scores(n, b, g):
        q4 = _stack_heads(lambda s: q_ref[b, :, s], g)
        s_scr[n % 2] = _dot_nt(q4, k_ref[b, :, g * HEAD_DIM:(g + 1) * HEAD_DIM])

    scores(0, *units[0])
    for n, (b, g) in enumerate(units):
        if n + 1 < len(units):
            scores(n + 1, *units[n + 1])
        c = n % 2
        _sink_softmax_rows(s_scr.at[c], p_scr.at[c], den_scr.at[c], m_scr.at[c], sink_ref, g * Q_PER_KV, t, ATTN_ROWS)
        o = _dot(p_scr[c], v_ref[b, :, g * HEAD_DIM:(g + 1) * HEAD_DIM]) / den_scr[c]
        for j in range(Q_PER_KV):
            hs = slice((g * Q_PER_KV + j) * HEAD_DIM, (g * Q_PER_KV + j + 1) * HEAD_DIM)
            o_ref[b, :, hs] = o[j * t:(j + 1) * t].astype(BF16)


def _ctx_attn(sink, q, k, v):
    b, t = q.shape[0], q.shape[1]
    bt = 4
    blk = lambda w: pl.BlockSpec((bt, t, w), lambda i: (i, 0, 0))
    return pl.pallas_call(
        _ctx_attn_kernel,
        grid=(b // bt,),
        in_specs=[pl.BlockSpec(memory_space=pltpu.SMEM), blk(ATTN_WIDTH), blk(KV_WIDTH), blk(KV_WIDTH)],
        out_specs=blk(ATTN_WIDTH),
        out_shape=jax.ShapeDtypeStruct((b, t, ATTN_WIDTH), BF16),
        scratch_shapes=[pltpu.VMEM((N_KV_HEADS, Q_PER_KV * t, t), F32),
                        pltpu.VMEM((N_KV_HEADS, Q_PER_KV * t, t), BF16),
                        pltpu.VMEM((N_KV_HEADS, Q_PER_KV * t, LANES), F32),
                        pltpu.VMEM((N_KV_HEADS, Q_PER_KV * t, LANES), F32)],
        compiler_params=_params(),
        name="ctx_attn",
    )(sink, q, k, v)


def _band_attn_kernel(sink_ref, q_ref, kp_ref, kc_ref, kn_ref, vp_ref, vc_ref, vn_ref,
                      ck32_ref, cv32_ref, o_ref, ck_ref, cv_ref, s_scr, p_scr, den_scr, m_scr):
    i = pl.program_id(1)
    nb = pl.num_programs(1)
    past = ck_ref.shape[0]

    @pl.when(i == 0)
    def _():
        for g in range(N_KV_HEADS):
            sl = slice(g * HEAD_DIM, (g + 1) * HEAD_DIM)
            ck_ref[:, sl] = ck32_ref[:, g, :].astype(BF16)
            cv_ref[:, sl] = cv32_ref[:, g, :].astype(BF16)

    rows = Q_PER_KV * BLOCK

    qrow = lax.broadcasted_iota(jnp.int32, (rows, BLOCK), 0) & (BLOCK - 1)
    col = lax.broadcasted_iota(jnp.int32, (rows, BLOCK), 1)
    in_prev = col >= qrow + jnp.where(i > 0, 0, 2 * BLOCK)
    in_next = col <= qrow - jnp.where(i < nb - 1, 0, 2 * BLOCK)

    def scores(g):
        sl = slice(g * HEAD_DIM, (g + 1) * HEAD_DIM)
        q4 = _stack_heads(lambda s: q_ref[:, s], g)
        kloc = jnp.concatenate([kp_ref[:, sl], kc_ref[:, sl], kn_ref[:, sl]], axis=0)
        s_scr[g, :, 0:past] = _dot_nt(q4, ck_ref[:, sl])
        s_loc = _dot_nt(q4, kloc)
        s_scr[g, :, past:past + BLOCK] = jnp.where(in_prev, s_loc[:, 0:BLOCK], NEG)
        s_scr[g, :, past + BLOCK:past + 2 * BLOCK] = s_loc[:, BLOCK:2 * BLOCK]
        s_scr[g, :, past + 2 * BLOCK:] = jnp.where(in_next, s_loc[:, 2 * BLOCK:], NEG)

    def values(g):
        sl = slice(g * HEAD_DIM, (g + 1) * HEAD_DIM)
        vloc = jnp.concatenate([vp_ref[:, sl], vc_ref[:, sl], vn_ref[:, sl]], axis=0)
        o = (_dot(p_scr[g, :, 0:past], cv_ref[:, sl]) + _dot(p_scr[g, :, past:], vloc)) / den_scr[g]
        for j in range(Q_PER_KV):
            hs = slice((g * Q_PER_KV + j) * HEAD_DIM, (g * Q_PER_KV + j + 1) * HEAD_DIM)
            o_ref[:, hs] = o[j * BLOCK:(j + 1) * BLOCK].astype(BF16)

    for g in range(N_KV_HEADS):
        scores(g)
    for g in range(N_KV_HEADS):
        _sink_softmax_rows(s_scr.at[g], p_scr.at[g], den_scr.at[g], m_scr.at[g], sink_ref, g * Q_PER_KV, BLOCK, ATTN_ROWS)
        values(g)


def _band_attn(sink, q, k, v, ctx_k, ctx_v, l):
    b, t = q.shape[0], q.shape[1]
    nb = t // BLOCK
    past = ctx_k.shape[2]
    keys = past + 3 * BLOCK
    cur = lambda w: pl.BlockSpec((None, BLOCK, w), lambda bi, i: (bi, i, 0))
    prev = pl.BlockSpec((None, BLOCK, KV_WIDTH), lambda bi, i: (bi, jnp.maximum(i - 1, 0), 0))
    nxt = pl.BlockSpec((None, BLOCK, KV_WIDTH), lambda bi, i: (bi, jnp.minimum(i + 1, nb - 1), 0))
    ctx = pl.BlockSpec((None, None, past, N_KV_HEADS, HEAD_DIM), lambda bi, i: (bi, l, 0, 0, 0))
    return pl.pallas_call(
        _band_attn_kernel,
        grid=(b, nb),
        in_specs=[pl.BlockSpec(memory_space=pltpu.SMEM), cur(ATTN_WIDTH),
                  prev, cur(KV_WIDTH), nxt, prev, cur(KV_WIDTH), nxt, ctx, ctx],
        out_specs=cur(ATTN_WIDTH),
        out_shape=jax.ShapeDtypeStruct((b, t, ATTN_WIDTH), BF16),
        scratch_shapes=[pltpu.VMEM((past, KV_WIDTH), BF16),
                        pltpu.VMEM((past, KV_WIDTH), BF16),
                        pltpu.VMEM((N_KV_HEADS, Q_PER_KV * BLOCK, keys), F32),
                        pltpu.VMEM((N_KV_HEADS, Q_PER_KV * BLOCK, keys), BF16),
                        pltpu.VMEM((N_KV_HEADS, Q_PER_KV * BLOCK, LANES), F32),
                        pltpu.VMEM((N_KV_HEADS, Q_PER_KV * BLOCK, LANES), F32)],
        compiler_params=_params(),
        name="band_attn",
    )(sink, q, k, k, k, v, v, v, ctx_k, ctx_v)


def _lru_kernel(xr_ref, gx_ref, cw_ref, cb_ref, lam_ref, wg_ref, bg_ref, h0_ref, *rest, seq_len, seg, groups):
    n_cast = (len(rest) - 8) // 2
    y_ref, st_ref = rest[n_cast:n_cast + 2]
    xpad, ldec, af, bf, ab, bb = rest[2 * n_cast + 2:]
    for src, dst in zip(rest[:n_cast], rest[n_cast + 2:2 * n_cast + 2]):
        dst[...] = src[...].astype(BF16)
    t = seq_len
    n_slab = xr_ref.shape[1] // LANES
    rc = min(LRU_CHUNK, t)
    n_chunks = t // rc
    n_seg = SUBLANES * groups
    gate_slabs = min(GATE_SLABS, n_slab)
    rows = n_seg * seg

    for s in range(n_slab):
        xpad[s, 0:SUBLANES, :] = jnp.zeros((SUBLANES, LANES), F32)
        xpad[s, SUBLANES + t:2 * SUBLANES + t, :] = jnp.zeros((SUBLANES, LANES), F32)
        xpad[s, SUBLANES:SUBLANES + t, :] = xr_ref[:, s * LANES:(s + 1) * LANES]
        af[s, t:rows, :] = jnp.ones((rows - t, LANES), F32)
        ab[s, t:rows, :] = jnp.ones((rows - t, LANES), F32)
        bf[s, t:rows, :] = jnp.zeros((rows - t, LANES), F32)
        bb[s, t:rows, :] = jnp.zeros((rows - t, LANES), F32)

    ldec[...] = (-0.5 * LRU_C) * _softplus(-lam_ref[...])

    def gates_chunk(idx, carry):
        sp = idx // n_chunks
        r0 = pl.multiple_of((idx - sp * n_chunks) * rc, rc)
        base = r0 + SUBLANES - CONV_LEFT
        for k in range(gate_slabs):
            s = sp * gate_slabs + k
            cw = cw_ref[s]
            xc = cw[0:1, :] * xpad[s, pl.ds(base, rc, stride=1), :]
            for tap in range(1, CONV_W):
                xc = xc + cw[tap:tap + 1, :] * xpad[s, pl.ds(base + tap, rc, stride=1), :]
            xc = xc + cb_ref[s]
            half_x = 0.5 * xc
            gh = _dot(half_x.astype(BF16), wg_ref[s]) + 0.5 * bg_ref[s]
            hd = ldec[s]
            for d, (a_scr, b_scr) in enumerate(((af, bf), (ab, bb))):
                off = 2 * d * LANES
                th_r = jnp.tanh(gh[:, off:off + LANES])
                th_i = jnp.tanh(gh[:, off + LANES:off + 2 * LANES])
                log_a = hd[d:d + 1, :] * th_r + hd[d:d + 1, :]
                a = jnp.exp(log_a)
                u = jnp.tanh(log_a) * (-1.0 - a * a)
                root = jnp.where(u == 0.0, 0.0, u * lax.rsqrt(u))
                a_scr[s, pl.ds(r0, rc), :] = a
                b_scr[s, pl.ds(r0, rc), :] = root * (half_x * th_i + half_x)
        return carry

    lax.fori_loop(0, (n_slab // gate_slabs) * n_chunks, gates_chunk, 0)

    def seg_rows(g, i):
        return pl.ds(g * SUBLANES * seg + i, SUBLANES, stride=seg)

    ones = jnp.ones((SUBLANES, LANES), F32)
    zeros = jnp.zeros((SUBLANES, LANES), F32)
    slabs_per_pass = max(1, SCAN_CHAINS // groups)
    chains = [(k, g) for k in range(slabs_per_pass) for g in range(groups)]

    def scan_slabs(sg, carry):
        s0 = sg * slabs_per_pass

        totals = [(ones, zeros, ones, zeros) for _ in chains]
        for i in range(seg):
            ib = seg - 1 - i
            for n, (k, g) in enumerate(chains):
                pf, hf, pb, hb = totals[n]
                a = af[s0 + k, seg_rows(g, i), :]
                a2 = ab[s0 + k, seg_rows(g, ib), :]
                totals[n] = (pf * a, a * hf + bf[s0 + k, seg_rows(g, i), :],
                             pb * a2, a2 * hb + bb[s0 + k, seg_rows(g, ib), :])

        starts = {}
        for k in range(slabs_per_pass):
            c = h0_ref[s0 + k, 0:1, :]
            for g in range(groups):
                pf, hf, _, _ = totals[k * groups + g]
                fwd = []
                for j in range(SUBLANES):
                    fwd.append(c)
                    c = hf[j:j + 1] + pf[j:j + 1] * c
                starts[(k, g, 0)] = jnp.concatenate(fwd, axis=0)
            st_ref[s0 + k, 0:1, :] = c
            c = h0_ref[s0 + k, 1:2, :]
            for g in reversed(range(groups)):
                _, _, pb, hb = totals[k * groups + g]
                bwd = [None] * SUBLANES
                for j in reversed(range(SUBLANES)):
                    bwd[j] = c
                    c = hb[j:j + 1] + pb[j:j + 1] * c
                starts[(k, g, 1)] = jnp.concatenate(bwd, axis=0)
            st_ref[s0 + k, 1:2, :] = c

        state = [(starts[(k, g, 0)], starts[(k, g, 1)]) for (k, g) in chains]
        for i in range(seg):
            ib = seg - 1 - i
            for n, (k, g) in enumerate(chains):
                hf, hb = state[n]
                hf = af[s0 + k, seg_rows(g, i), :] * hf + bf[s0 + k, seg_rows(g, i), :]
                hb = ab[s0 + k, seg_rows(g, ib), :] * hb + bb[s0 + k, seg_rows(g, ib), :]
                bf[s0 + k, seg_rows(g, i), :] = hf
                bb[s0 + k, seg_rows(g, ib), :] = hb
                state[n] = (hf, hb)
        return carry

    lax.fori_loop(0, n_slab // slabs_per_pass, scan_slabs, 0)

    def out_chunk(ci, carry):
        r0 = pl.multiple_of(ci * rc, rc)
        for s in range(n_slab):
            sl = slice(s * LANES, (s + 1) * LANES)
            hsum = bf[s, pl.ds(r0, rc), :] + bb[s, pl.ds(r0, rc), :]
            y_ref[pl.ds(r0, rc), sl] = (hsum * gx_ref[pl.ds(r0, rc), sl]).astype(BF16)
        return carry

    lax.fori_loop(0, n_chunks, out_chunk, 0)


def _lru(xr, gx, h0, l, seq_len, conv_w, conv_b, lam, wg, bg, cb, groups, cast=()):
    n = xr.shape[0]
    b = n // seq_len
    n_slab = cb // LANES
    nj = D_RNN // cb
    steps = b * nj
    cast_in = [pl.BlockSpec((None, w.shape[1] // steps, w.shape[2]), lambda bi, j: (l, bi * nj + j, 0)) for w in cast]
    cast_out = [pl.BlockSpec((w.shape[1] // steps, w.shape[2]), lambda bi, j: (bi * nj + j, 0)) for w in cast]
    seg = seq_len // (SUBLANES * groups) + SEG_PAD
    assert n_slab % min(GATE_SLABS, n_slab) == 0 and n_slab % max(1, SCAN_CHAINS // groups) == 0
    rows = SUBLANES * groups * seg
    col = lambda bi, j: (bi, j)
    par = lambda r, w: pl.BlockSpec((None, n_slab, r, w), lambda bi, j: (l, j, 0, 0))
    state = pl.BlockSpec((None, n_slab, 2, LANES), lambda bi, j: (bi, j, 0, 0))
    scan_buf = pltpu.VMEM((n_slab, rows, LANES), F32)
    return pl.pallas_call(
        functools.partial(_lru_kernel, seq_len=seq_len, seg=seg, groups=groups),
        grid=(b, D_RNN // cb),
        in_specs=[
            pl.BlockSpec((seq_len, cb), col),
            pl.BlockSpec((seq_len, cb), col),
            par(CONV_W, LANES), par(1, LANES), par(2, LANES),
            par(RNN_BLOCK_W, 4 * RNN_BLOCK_W), par(1, 4 * RNN_BLOCK_W),
            state,
        ] + cast_in,
        out_specs=[pl.BlockSpec((seq_len, cb), col), state] + cast_out,
        out_shape=[
            jax.ShapeDtypeStruct((n, D_RNN), BF16),
            jax.ShapeDtypeStruct((b, N_RNN_BLOCKS, 2, LANES), F32),
        ] + [jax.ShapeDtypeStruct(w.shape[1:], BF16) for w in cast],
        scratch_shapes=[pltpu.VMEM((n_slab, seq_len + 2 * SUBLANES, LANES), F32),
                        pltpu.VMEM((n_slab, 2, LANES), F32),
                        scan_buf, scan_buf, scan_buf, scan_buf],
        compiler_params=_params(),
        name="rglru",
    )(xr, gx, conv_w, conv_b, lam, wg, bg, h0, *cast)


def _post_kernel(x_ref, attn_ref, lru_ref, g_ref, mod_ref, nw_ref, wa_ref, wl_ref, wo_ref,
                 w1_ref, w2_ref, *rest):
    n_cast = (len(rest) - 1) // 2
    o_ref = rest[n_cast]
    for src, dst in zip(rest[:n_cast], rest[n_cast + 1:]):
        dst[...] = src[...].astype(BF16)
    half = x_ref.shape[0] // ROW_SPLIT
    for r in range(ROW_SPLIT):
        rows = slice(r * half, (r + 1) * half)
        a = _dot(attn_ref[rows, :], wa_ref[...])
        b = _dot(lru_ref[rows, :], wl_ref[...])
        merged = _sigmoid(g_ref[rows, 0:D_MODEL]) * a + _sigmoid(g_ref[rows, D_MODEL:]) * b
        x = x_ref[rows, :] + mod_ref[2:3, :] * _dot(merged.astype(BF16), wo_ref[...])
        h = (_rms(x) * nw_ref[...] * (1.0 + mod_ref[4:5, :]) + mod_ref[3:4, :]).astype(BF16)
        ff = D_MODEL
        acc = None
        for c in range(D_FF // ff):
            hid = jnp.maximum(_dot(h, w1_ref[:, c * ff:(c + 1) * ff]), 0.0)
            part = _dot((hid * hid).astype(BF16), w2_ref[c * ff:(c + 1) * ff, :])
            acc = part if acc is None else acc + part
        o_ref[rows, :] = x + mod_ref[5:6, :] * acc


def _post(x, attn, lru, gates, mod, l, cond_of_tile, norm2_w, w_attn_o, w_lru_o, w_out, w1, w2, in_place, cast=()):
    n = x.shape[0]
    tm = ROW_TILE
    steps = n // tm
    row = lambda i: (i, 0)
    cast_in = [pl.BlockSpec((None, w.shape[1] // steps, w.shape[2]), lambda i: (l + 1, i, 0)) for w in cast]
    cast_out = [pl.BlockSpec((w.shape[1] // steps, w.shape[2]), row) for w in cast]
    outs = pl.pallas_call(
        _post_kernel,
        grid=(steps,),
        in_specs=[
            pl.BlockSpec((tm, D_MODEL), row),
            pl.BlockSpec((tm, ATTN_WIDTH), row),
            pl.BlockSpec((tm, D_RNN), row),
            pl.BlockSpec((tm, 2 * D_MODEL), row),
            pl.BlockSpec((None, None, N_MOD, D_MODEL), lambda i: (l, cond_of_tile(i), 0, 0)),
            _resident((None, 1, D_MODEL), lambda i: (l, 0, 0)),
            _layer_weight(w_attn_o, l), _layer_weight(w_lru_o, l), _layer_weight(w_out, l),
            _layer_weight(w1, l), _layer_weight(w2, l),
        ] + cast_in,
        out_specs=[pl.BlockSpec((tm, D_MODEL), row)] + cast_out,
        out_shape=[jax.ShapeDtypeStruct((n, D_MODEL), F32)]
        + [jax.ShapeDtypeStruct(w.shape[1:], BF16) for w in cast],
        input_output_aliases={0: 0} if in_place else {},
        compiler_params=_params(),
        name="post",
    )(x, attn, lru, gates, mod, norm2_w, w_attn_o, w_lru_o, w_out, w1, w2, *cast)
    return outs[0], outs[1:]


def _rope_tables(n_tokens):
    rows = n_tokens // GRID_W
    row = jnp.broadcast_to(jnp.arange(rows)[:, None], (rows, GRID_W)).reshape(-1).astype(F32)
    col = jnp.broadcast_to(jnp.arange(GRID_W)[None, :], (rows, GRID_W)).reshape(-1).astype(F32)
    freqs = ROPE_THETA ** (-jnp.arange(0, ROPE_AXIS_DIM, 2, dtype=F32) / ROPE_AXIS_DIM)
    ang_r = row[:, None] * freqs
    ang_c = col[:, None] * freqs
    cr, sr, cc, sc = jnp.cos(ang_r), jnp.sin(ang_r), jnp.cos(ang_c), jnp.sin(ang_c)
    return (jnp.concatenate([cr, cr, cc, cc], axis=-1),
            jnp.concatenate([-sr, sr, -sc, sc], axis=-1))


def kernel(x_prompt, x_sample, cache_k, cache_v, state_lru, c, c_ctx, ada_w, ada_b, norm1_w, norm2_w, w_in, q_norm_w, k_norm_w, attn_sink, conv_w, conv_b, lru_lambda, lru_wa, lru_ba, lru_wi, lru_bi, w_attn_o, w_lru_o, w_out, mlp_w1, mlp_w2):
    bp, tp, _ = x_prompt.shape
    bs, ts, _ = x_sample.shape
    past = cache_k.shape[2]
    assert bs + 1 <= N_COND and ts % ROW_TILE == 0 and (bp * tp) % ROW_TILE == 0

    cond = jnp.zeros((N_COND, D_MODEL), F32).at[0].set(c_ctx).at[1:1 + bs].set(c)
    mod = _adaln(cond, ada_w, ada_b).reshape(DEPTH, N_COND, N_MOD, D_MODEL)
    tiles_per_sample = ts // ROW_TILE
    cond_prompt = lambda i: 0
    cond_sample = lambda i: 1 + i // tiles_per_sample

    w_in_b = w_in[0].astype(BF16)
    post_f32 = (w_attn_o, w_lru_o, w_out, mlp_w1, mlp_w2)
    post_w = None
    wg = jnp.concatenate([lru_wa[:, 0], lru_wi[:, 0], lru_wa[:, 1], lru_wi[:, 1]], axis=-1).astype(BF16)
    blk = lambda v: v.reshape(DEPTH, N_RNN_BLOCKS, 1, RNN_BLOCK_W)
    bg = jnp.concatenate([blk(lru_ba[:, 0]), blk(lru_bi[:, 0]), blk(lru_ba[:, 1]), blk(lru_bi[:, 1])], axis=-1)
    rope_tabs = _rope_tables(ts)

    n1 = norm1_w.reshape(DEPTH, 1, D_MODEL)
    n2 = norm2_w.reshape(DEPTH, 1, D_MODEL)
    qw = q_norm_w.reshape(DEPTH, 1, HEAD_DIM)
    kw = k_norm_w.reshape(DEPTH, 1, HEAD_DIM)
    by_block = lambda v: jnp.swapaxes(v.reshape(DEPTH, -1, N_RNN_BLOCKS, RNN_BLOCK_W), 1, 2)
    cvw = by_block(conv_w)
    cvb = by_block(conv_b)
    lam = by_block(lru_lambda)
    state_by_block = lambda v: jnp.swapaxes(v.reshape(-1, 2, N_RNN_BLOCKS, RNN_BLOCK_W), 1, 2)
    zero_state = jnp.zeros((bp, N_RNN_BLOCKS, 2, RNN_BLOCK_W), F32)

    xp = x_prompt.reshape(bp * tp, D_MODEL)
    xs = x_sample.reshape(bs * ts, D_MODEL)
    kv_cache, new_s = None, []
    for l in range(DEPTH):
        in_place = l > 0
        q, k, v, xr, gx, gates, *kv_cache = _inproj(xp, mod, l, cond_prompt, n1, w_in_b, qw, kw, kv_cache=kv_cache,
                                                    kv_seq_len=tp, tm=ROW_TILE if kv_cache else tp)
        attn = _ctx_attn(attn_sink[l], q.reshape(bp, tp, -1), k.reshape(bp, tp, -1), v.reshape(bp, tp, -1))
        lru, st, *first_w = _lru(xr, gx, zero_state, l, tp, cvw, cvb, lam, wg, bg, D_RNN, 1,
                                 cast=post_f32 if post_w is None else ())
        post_w = post_w or first_w
        more = l + 1 < DEPTH
        xp, next_w_in = _post(xp, attn.reshape(bp * tp, -1), lru, gates, mod, l, cond_prompt, n2,
                              *post_w, in_place, cast=(w_in,) if more else ())
        new_s.append(jnp.swapaxes(st, 1, 2).reshape(bp, 2, D_RNN))
        q, k, v, xr, gx, gates = _inproj(xs, mod, l, cond_sample, n1, w_in_b, qw, kw, rope_tabs=rope_tabs)
        attn = _band_attn(attn_sink[l], q.reshape(bs, ts, -1), k.reshape(bs, ts, -1),
                          v.reshape(bs, ts, -1), cache_k, cache_v, l)
        lru, _ = _lru(xr, gx, state_by_block(state_lru[:, l]), l, ts, cvw, cvb, lam, wg, bg, 512, 2)[:2]
        xs, next_post_w = _post(xs, attn.reshape(bs * ts, -1), lru, gates, mod, l, cond_sample, n2,
                                *post_w, in_place,
                                cast=post_f32 if more else ())
        if more:
            (w_in_b,), post_w = next_w_in, next_post_w

    new_k_arr, new_v_arr = kv_cache
    return (xp.reshape(bp, tp, D_MODEL), xs.reshape(bs, ts, D_MODEL), new_k_arr, new_v_arr,
            jnp.stack(new_s, axis=1))
```

```python
import functools

import jax
import jax.numpy as jnp
from jax import lax
from jax.experimental import pallas as pl
from jax.experimental.pallas import tpu as pltpu

D_MODEL = 1024
DEPTH = 4
GRID_W = 64
HEAD_DIM = 128
N_Q_HEADS = 8
N_KV_HEADS = 2
Q_PER_KV = N_Q_HEADS // N_KV_HEADS
ATTN_WIDTH = N_Q_HEADS * HEAD_DIM
KV_WIDTH = N_KV_HEADS * HEAD_DIM
WINDOW = 128
BLOCK = 128
SCALE = HEAD_DIM ** -0.5
ROPE_AXIS_DIM = HEAD_DIM // 2
ROPE_THETA = 10000.0
D_RNN = D_MODEL
N_RNN_BLOCKS = 8
RNN_BLOCK_W = D_RNN // N_RNN_BLOCKS
CONV_W = 4
CONV_LEFT = 2
LRU_C = 8.0
D_FF = 4 * D_MODEL
N_MOD = 6
EPS = 1e-6
NEG = -1e30
SPLIT_Q = ATTN_WIDTH
SPLIT_K = SPLIT_Q + KV_WIDTH
SPLIT_V = SPLIT_K + KV_WIDTH
SPLIT_XR = SPLIT_V + D_RNN
SPLIT_XG = SPLIT_XR + D_RNN
IN_WIDTH = SPLIT_XG + 2 * D_MODEL

LANES = 128
SUBLANES = 8
VMEM_LIMIT = 56 * 1024 * 1024

ROW_TILE = 512
ROW_SPLIT = 2
N_COND = 8
SEG_PAD = 4
ATTN_ROWS = 64
LOG2E = 1.4426950408889634
ADALN_COLS = 1536
CTX_SEQS = 4
LRU_CHUNK = 256
LRU_CTX = (D_RNN, 1)
LRU_LATENT = (512, 2)
SCAN_CHAINS = 4
GATE_SLABS = 4

F32 = jnp.float32
BF16 = jnp.bfloat16


def _sigmoid(x):
    return 0.5 * jnp.tanh(0.5 * x) + 0.5


def _gelu_tanh(x):
    c = 0.7978845608028654
    return 0.5 * x * (1.0 + jnp.tanh(c * (x + 0.044715 * (x * x * x))))


def _softplus(x):
    return jnp.maximum(x, 0.0) + jnp.log1p(jnp.exp(-jnp.abs(x)))


def _dot(a, b):
    return jnp.dot(a, b, preferred_element_type=F32)


def _dot_nt(a, b):
    return lax.dot_general(a, b, (((1,), (1,)), ((), ())), preferred_element_type=F32)


def _rms(x):
    return x * lax.rsqrt(jnp.mean(x * x, axis=-1, keepdims=True) + EPS)


def _resident(shape, index_map):
    return pl.BlockSpec(shape, index_map, pipeline_mode=pl.Buffered(1))


def _layer_weight(w, l):
    if w.ndim == 3:
        return _resident((None,) + w.shape[1:], lambda i: (l, 0, 0))
    return _resident(w.shape, lambda i: (0, 0))


def _params():
    return pltpu.CompilerParams(vmem_limit_bytes=VMEM_LIMIT)


def _adaln_kernel(cond_ref, w_ref, b_ref, o_ref):
    cnd = cond_ref[...]
    act = (cnd * _sigmoid(cnd)).astype(BF16)
    o_ref[...] = _dot(act, w_ref[...].astype(BF16)) + b_ref[...]


def _adaln(cond, ada_w, ada_b):
    tn = ADALN_COLS
    width = N_MOD * D_MODEL
    return pl.pallas_call(
        _adaln_kernel,
        grid=(DEPTH, width // tn),
        in_specs=[
            pl.BlockSpec((N_COND, D_MODEL), lambda l, j: (0, 0)),
            pl.BlockSpec((None, D_MODEL, tn), lambda l, j: (l, 0, j)),
            pl.BlockSpec((None, 1, tn), lambda l, j: (l, 0, j)),
        ],
        out_specs=pl.BlockSpec((None, N_COND, tn), lambda l, j: (l, 0, j)),
        out_shape=jax.ShapeDtypeStruct((DEPTH, N_COND, width), F32),
        compiler_params=_params(),
        name="adaln",
    )(cond, ada_w, ada_b.reshape(DEPTH, 1, width))


def _rope(x, cs, sn):
    lane = lax.broadcasted_iota(jnp.int32, x.shape, 1)
    half = ROPE_AXIS_DIM // 2
    swapped = jnp.where((lane & (ROPE_AXIS_DIM - 1)) < half,
                        pltpu.roll(x, HEAD_DIM - half, 1), pltpu.roll(x, half, 1))
    return x * cs + swapped * sn


def _inproj_kernel(*refs, rope, keep_kv, row_split, seq_len, slot):
    x_ref, mod_ref, nw_ref, w_ref, qw_ref, kw_ref = refs[:6]
    refs = refs[6:]
    if rope:
        cs_ref, sn_ref = refs[:2]
        refs = refs[2:]
    if keep_kv:
        refs = refs[-8:]
        k32_ref, v32_ref = refs[6:8]
    q_ref, k_ref, v_ref, xr_ref, gx_ref, g_ref = refs[:6]

    half = x_ref.shape[0] // row_split
    hs = []
    for r in range(row_split):
        rows = slice(r * half, (r + 1) * half)
        y = _rms(x_ref[rows, :]) * nw_ref[...]
        hs.append((y * (1.0 + mod_ref[1:2, :]) + mod_ref[0:1, :]).astype(BF16))
    if keep_kv and slot is not None:
        for ref in (k32_ref, v32_ref):
            for other in range(DEPTH):
                if other != slot:
                    ref[:, other] = jnp.zeros((ref.shape[0],) + ref.shape[2:], F32)
    for r in range(row_split):
        rows = slice(r * half, (r + 1) * half)
        h = hs[r]
        if keep_kv:
            kv_at = (r * half // seq_len,) + (() if slot is None else (slot,)) + (pl.ds(r * half % seq_len, half),)

        def head(z, w):
            o = _rms(z) * w
            return _rope(o, cs_ref[rows, :], sn_ref[rows, :]) if rope else o

        zq = _dot(h, w_ref[:, 0:SPLIT_Q])
        for hd in range(N_Q_HEADS):
            sl = slice(hd * HEAD_DIM, (hd + 1) * HEAD_DIM)
            q_ref[rows, sl] = head(zq[:, sl], qw_ref[...]).astype(BF16)

        zkv = _dot(h, w_ref[:, SPLIT_Q:SPLIT_V])
        for hd in range(N_KV_HEADS):
            sl = slice(hd * HEAD_DIM, (hd + 1) * HEAD_DIM)
            kn = head(zkv[:, sl], kw_ref[...])
            k_ref[rows, sl] = kn.astype(BF16)
            if keep_kv:
                k32_ref[kv_at + (hd, slice(None))] = kn
        zv = zkv[:, KV_WIDTH:2 * KV_WIDTH]
        v_ref[rows, :] = zv.astype(BF16)
        if keep_kv:
            for hd in range(N_KV_HEADS):
                v32_ref[kv_at + (hd, slice(None))] = zv[:, hd * HEAD_DIM:(hd + 1) * HEAD_DIM]

        xr_ref[rows, :] = _dot(h, w_ref[:, SPLIT_V:SPLIT_XR])
        gx_ref[rows, :] = _gelu_tanh(_dot(h, w_ref[:, SPLIT_XR:SPLIT_XG]))
        g_ref[rows, 0:D_MODEL] = _dot(h, w_ref[:, SPLIT_XG:SPLIT_XG + D_MODEL])
        g_ref[rows, D_MODEL:] = _dot(h, w_ref[:, SPLIT_XG + D_MODEL:])


def _inproj(x, mod, l, cond_of_tile, norm1_w, w_in, q_norm_w, k_norm_w, rope_tabs=None, kv_cache=None,
            kv_seq_len=None, tm=ROW_TILE):
    n = x.shape[0]
    rope = rope_tabs is not None
    keep_kv = kv_seq_len is not None
    creates = keep_kv and kv_cache is None
    row = lambda i: (i, 0)
    in_specs = [
        pl.BlockSpec((tm, D_MODEL), row),
        pl.BlockSpec((None, None, N_MOD, D_MODEL), lambda i: (l, cond_of_tile(i), 0, 0)),
        _resident((None, 1, D_MODEL), lambda i: (l, 0, 0)),
        _layer_weight(w_in, l),
        _resident((None, 1, HEAD_DIM), lambda i: (l, 0, 0)),
        _resident((None, 1, HEAD_DIM), lambda i: (l, 0, 0)),
    ]
    args = [x, mod, norm1_w, w_in, q_norm_w, k_norm_w]
    if rope:
        tiles_per_seq = rope_tabs[0].shape[0] // tm
        tab = pl.BlockSpec((tm, HEAD_DIM), lambda i: (i % tiles_per_seq, 0))
        in_specs += [tab, tab]
        args += list(rope_tabs)
    out_shape = [
        jax.ShapeDtypeStruct((n, ATTN_WIDTH), BF16),
        jax.ShapeDtypeStruct((n, KV_WIDTH), BF16),
        jax.ShapeDtypeStruct((n, KV_WIDTH), BF16),
        jax.ShapeDtypeStruct((n, D_RNN), F32),
        jax.ShapeDtypeStruct((n, D_RNN), F32),
        jax.ShapeDtypeStruct((n, 2 * D_MODEL), F32),
    ]
    out_specs = [
        pl.BlockSpec((tm, ATTN_WIDTH), row),
        pl.BlockSpec((tm, KV_WIDTH), row),
        pl.BlockSpec((tm, KV_WIDTH), row),
        pl.BlockSpec((tm, D_RNN), row),
        pl.BlockSpec((tm, D_RNN), row),
        pl.BlockSpec((tm, 2 * D_MODEL), row),
    ]
    aliases = {}
    if keep_kv:
        seqs = tm // kv_seq_len
        assert tm % kv_seq_len == 0 and n % kv_seq_len == 0
        kv_shape = (n // kv_seq_len, DEPTH, kv_seq_len, N_KV_HEADS, HEAD_DIM)
        out_shape += [jax.ShapeDtypeStruct(kv_shape, F32)] * 2
        if creates:
            kv_block = pl.BlockSpec((seqs, DEPTH, kv_seq_len, N_KV_HEADS, HEAD_DIM), lambda i: (i, 0, 0, 0, 0))
        else:
            kv_block = pl.BlockSpec((seqs, None, kv_seq_len, N_KV_HEADS, HEAD_DIM), lambda i: (i, l, 0, 0, 0))
        out_specs += [kv_block] * 2
        if kv_cache is not None:
            aliases = {len(args): 6, len(args) + 1: 7}
            in_specs += [pl.BlockSpec(memory_space=pl.ANY)] * 2
            args += list(kv_cache)
    return pl.pallas_call(
        functools.partial(_inproj_kernel, rope=rope, keep_kv=keep_kv, row_split=ROW_SPLIT,
                          seq_len=kv_seq_len, slot=l if creates else None),
        grid=(n // tm,),
        in_specs=in_specs,
        out_specs=out_specs,
        out_shape=out_shape,
        input_output_aliases=aliases,
        compiler_params=_params(),
        name="inproj_rope" if rope else "inproj",
    )(*args)


def _stack_heads(q_at, g):
    return jnp.concatenate(
        [q_at(slice((g * Q_PER_KV + j) * HEAD_DIM, (g * Q_PER_KV + j + 1) * HEAD_DIM))
         for j in range(Q_PER_KV)], axis=0)


def _sink_softmax_rows(s_scr, p_scr, den_scr, m_scr, sink_ref, head0, rows_per_head, rch, n_heads=Q_PER_KV):
    c1 = SCALE * LOG2E
    cols = [slice(c * LANES, (c + 1) * LANES) for c in range(s_scr.shape[1] // LANES)]
    chunks = [(pl.ds(ci * rch, rch), head0 + ci * rch // rows_per_head)
              for ci in range(n_heads * rows_per_head // rch)]
    for rws, head in chunks:
        mx = s_scr[rws, cols[0]]
        for c in cols[1:]:
            mx = jnp.maximum(mx, s_scr[rws, c])
        row_max = jnp.broadcast_to(jnp.max(mx, axis=-1, keepdims=True), mx.shape)
        m_scr[rws, :] = jnp.maximum(row_max * SCALE, sink_ref[head])
    for rws, head in chunks:
        m = m_scr[rws, :]
        mr = m * (1.0 / SCALE)
        acc = None
        for c in cols:
            e = jnp.exp2((s_scr[rws, c] - mr) * c1)
            p_scr[rws, c] = e.astype(BF16)
            acc = e if acc is None else acc + e
        row_sum = jnp.broadcast_to(jnp.sum(acc, axis=-1, keepdims=True), acc.shape)
        den_scr[rws, :] = row_sum + jnp.exp(sink_ref[head] - m)


def _ctx_attn_kernel(sink_ref, q_ref, k_ref, v_ref, o_ref, s_scr, p_scr, den_scr, m_scr):
    bt, t = q_ref.shape[0], q_ref.shape[1]
    units = [(b, g) for b in range(bt) for g in range(N_KV_HEADS)]

    def scores(n, b, g):
        q4 = _stack_heads(lambda s: q_ref[b, :, s], g)
        s_scr[n % 2] = _dot_nt(q4, k_ref[b, :, g * HEAD_DIM:(g + 1) * HEAD_DIM])

    scores(0, *units[0])
    for n, (b, g) in enumerate(units):
        if n + 1 < len(units):
            scores(n + 1, *units[n + 1])
        c = n % 2
        _sink_softmax_rows(s_scr.at[c], p_scr.at[c], den_scr.at[c], m_scr.at[c], sink_ref, g * Q_PER_KV, t, ATTN_ROWS)
        o = _dot(p_scr[c], v_ref[b, :, g * HEAD_DIM:(g + 1) * HEAD_DIM]) / den_scr[c]
        for j in range(Q_PER_KV):
            hs = slice((g * Q_PER_KV + j) * HEAD_DIM, (g * Q_PER_KV + j + 1) * HEAD_DIM)
            o_ref[b, :, hs] = o[j * t:(j + 1) * t].astype(BF16)


def _ctx_attn(sink, q, k, v):
    b, t = q.shape[0], q.shape[1]
    bt = CTX_SEQS
    blk = lambda w: pl.BlockSpec((bt, t, w), lambda i: (i, 0, 0))
    return pl.pallas_call(
        _ctx_attn_kernel,
        grid=(b // bt,),
        in_specs=[pl.BlockSpec(memory_space=pltpu.SMEM), blk(ATTN_WIDTH), blk(KV_WIDTH), blk(KV_WIDTH)],
        out_specs=blk(ATTN_WIDTH),
        out_shape=jax.ShapeDtypeStruct((b, t, ATTN_WIDTH), BF16),
        scratch_shapes=[pltpu.VMEM((N_KV_HEADS, Q_PER_KV * t, t), F32),
                        pltpu.VMEM((N_KV_HEADS, Q_PER_KV * t, t), BF16),
                        pltpu.VMEM((N_KV_HEADS, Q_PER_KV * t, LANES), F32),
                        pltpu.VMEM((N_KV_HEADS, Q_PER_KV * t, LANES), F32)],
        compiler_params=_params(),
        name="ctx_attn",
    )(sink, q, k, v)


def _band_attn_kernel(sink_ref, q_ref, kp_ref, kc_ref, kn_ref, vp_ref, vc_ref, vn_ref,
                      ck32_ref, cv32_ref, o_ref, ck_ref, cv_ref, s_scr, p_scr, den_scr, m_scr):
    i = pl.program_id(1)
    nb = pl.num_programs(1)
    past = ck_ref.shape[0]

    @pl.when(i == 0)
    def _():
        for g in range(N_KV_HEADS):
            sl = slice(g * HEAD_DIM, (g + 1) * HEAD_DIM)
            ck_ref[:, sl] = ck32_ref[:, g, :].astype(BF16)
            cv_ref[:, sl] = cv32_ref[:, g, :].astype(BF16)

    rows = Q_PER_KV * BLOCK

    qrow = lax.broadcasted_iota(jnp.int32, (rows, BLOCK), 0) & (BLOCK - 1)
    col = lax.broadcasted_iota(jnp.int32, (rows, BLOCK), 1)
    in_prev = col >= qrow + jnp.where(i > 0, 0, 2 * BLOCK)
    in_next = col <= qrow - jnp.where(i < nb - 1, 0, 2 * BLOCK)

    def scores(g):
        sl = slice(g * HEAD_DIM, (g + 1) * HEAD_DIM)
        q4 = _stack_heads(lambda s: q_ref[:, s], g)
        kloc = jnp.concatenate([kp_ref[:, sl], kc_ref[:, sl], kn_ref[:, sl]], axis=0)
        s_scr[g, :, 0:past] = _dot_nt(q4, ck_ref[:, sl])
        s_loc = _dot_nt(q4, kloc)
        s_scr[g, :, past:past + BLOCK] = jnp.where(in_prev, s_loc[:, 0:BLOCK], NEG)
        s_scr[g, :, past + BLOCK:past + 2 * BLOCK] = s_loc[:, BLOCK:2 * BLOCK]
        s_scr[g, :, past + 2 * BLOCK:] = jnp.where(in_next, s_loc[:, 2 * BLOCK:], NEG)

    def values(g):
        sl = slice(g * HEAD_DIM, (g + 1) * HEAD_DIM)
        vloc = jnp.concatenate([vp_ref[:, sl], vc_ref[:, sl], vn_ref[:, sl]], axis=0)
        o = (_dot(p_scr[g, :, 0:past], cv_ref[:, sl]) + _dot(p_scr[g, :, past:], vloc)) / den_scr[g]
        for j in range(Q_PER_KV):
            hs = slice((g * Q_PER_KV + j) * HEAD_DIM, (g * Q_PER_KV + j + 1) * HEAD_DIM)
            o_ref[:, hs] = o[j * BLOCK:(j + 1) * BLOCK].astype(BF16)

    for g in range(N_KV_HEADS):
        scores(g)
    for g in range(N_KV_HEADS):
        _sink_softmax_rows(s_scr.at[g], p_scr.at[g], den_scr.at[g], m_scr.at[g], sink_ref, g * Q_PER_KV, BLOCK, ATTN_ROWS)
        values(g)


def _band_attn(sink, q, k, v, ctx_k, ctx_v, l):
    b, t = q.shape[0], q.shape[1]
    nb = t // BLOCK
    past = ctx_k.shape[2]
    keys = past + 3 * BLOCK
    cur = lambda w: pl.BlockSpec((None, BLOCK, w), lambda bi, i: (bi, i, 0))
    prev = pl.BlockSpec((None, BLOCK, KV_WIDTH), lambda bi, i: (bi, jnp.maximum(i - 1, 0), 0))
    nxt = pl.BlockSpec((None, BLOCK, KV_WIDTH), lambda bi, i: (bi, jnp.minimum(i + 1, nb - 1), 0))
    ctx = pl.BlockSpec((None, None, past, N_KV_HEADS, HEAD_DIM), lambda bi, i: (bi, l, 0, 0, 0))
    return pl.pallas_call(
        _band_attn_kernel,
        grid=(b, nb),
        in_specs=[pl.BlockSpec(memory_space=pltpu.SMEM), cur(ATTN_WIDTH),
                  prev, cur(KV_WIDTH), nxt, prev, cur(KV_WIDTH), nxt, ctx, ctx],
        out_specs=cur(ATTN_WIDTH),
        out_shape=jax.ShapeDtypeStruct((b, t, ATTN_WIDTH), BF16),
        scratch_shapes=[pltpu.VMEM((past, KV_WIDTH), BF16),
                        pltpu.VMEM((past, KV_WIDTH), BF16),
                        pltpu.VMEM((N_KV_HEADS, Q_PER_KV * BLOCK, keys), F32),
                        pltpu.VMEM((N_KV_HEADS, Q_PER_KV * BLOCK, keys), BF16),
                        pltpu.VMEM((N_KV_HEADS, Q_PER_KV * BLOCK, LANES), F32),
                        pltpu.VMEM((N_KV_HEADS, Q_PER_KV * BLOCK, LANES), F32)],
        compiler_params=_params(),
        name="band_attn",
    )(sink, q, k, k, k, v, v, v, ctx_k, ctx_v)


def _lru_kernel(xr_ref, gx_ref, cw_ref, cb_ref, lam_ref, wg_ref, bg_ref, h0_ref, *rest, seq_len, seg, groups):
    n_cast = (len(rest) - 8) // 2
    y_ref, st_ref = rest[n_cast:n_cast + 2]
    xpad, ldec, af, bf, ab, bb = rest[2 * n_cast + 2:]
    for src, dst in zip(rest[:n_cast], rest[n_cast + 2:2 * n_cast + 2]):
        dst[...] = src[...].astype(BF16)
    t = seq_len
    n_slab = xr_ref.shape[1] // LANES
    rc = min(LRU_CHUNK, t)
    n_chunks = t // rc
    n_seg = SUBLANES * groups
    gate_slabs = min(GATE_SLABS, n_slab)
    rows = n_seg * seg

    for s in range(n_slab):
        xpad[s, 0:SUBLANES, :] = jnp.zeros((SUBLANES, LANES), F32)
        xpad[s, SUBLANES + t:2 * SUBLANES + t, :] = jnp.zeros((SUBLANES, LANES), F32)
        xpad[s, SUBLANES:SUBLANES + t, :] = xr_ref[:, s * LANES:(s + 1) * LANES]
        af[s, t:rows, :] = jnp.ones((rows - t, LANES), F32)
        ab[s, t:rows, :] = jnp.ones((rows - t, LANES), F32)
        bf[s, t:rows, :] = jnp.zeros((rows - t, LANES), F32)
        bb[s, t:rows, :] = jnp.zeros((rows - t, LANES), F32)

    ldec[...] = (-0.5 * LRU_C) * _softplus(-lam_ref[...])

    def gates_chunk(idx, carry):
        sp = idx // n_chunks
        r0 = pl.multiple_of((idx - sp * n_chunks) * rc, rc)
        base = r0 + SUBLANES - CONV_LEFT
        for k in range(gate_slabs):
            s = sp * gate_slabs + k
            cw = cw_ref[s]
            xc = cw[0:1, :] * xpad[s, pl.ds(base, rc, stride=1), :]
            for tap in range(1, CONV_W):
                xc = xc + cw[tap:tap + 1, :] * xpad[s, pl.ds(base + tap, rc, stride=1), :]
            xc = xc + cb_ref[s]
            half_x = 0.5 * xc
            gh = _dot(half_x.astype(BF16), wg_ref[s]) + 0.5 * bg_ref[s]
            hd = ldec[s]
            for d, (a_scr, b_scr) in enumerate(((af, bf), (ab, bb))):
                off = 2 * d * LANES
                th_r = jnp.tanh(gh[:, off:off + LANES])
                th_i = jnp.tanh(gh[:, off + LANES:off + 2 * LANES])
                log_a = hd[d:d + 1, :] * th_r + hd[d:d + 1, :]
                a = jnp.exp(log_a)
                u = jnp.tanh(log_a) * (-1.0 - a * a)
                root = jnp.where(u == 0.0, 0.0, u * lax.rsqrt(u))
                a_scr[s, pl.ds(r0, rc), :] = a
                b_scr[s, pl.ds(r0, rc), :] = root * (half_x * th_i + half_x)
        return carry

    lax.fori_loop(0, (n_slab // gate_slabs) * n_chunks, gates_chunk, 0)

    def seg_rows(g, i):
        return pl.ds(g * SUBLANES * seg + i, SUBLANES, stride=seg)

    ones = jnp.ones((SUBLANES, LANES), F32)
    zeros = jnp.zeros((SUBLANES, LANES), F32)
    slabs_per_pass = max(1, SCAN_CHAINS // groups)
    chains = [(k, g) for k in range(slabs_per_pass) for g in range(groups)]

    def scan_slabs(sg, carry):
        s0 = sg * slabs_per_pass

        totals = [(ones, zeros, ones, zeros) for _ in chains]
        for i in range(seg):
            ib = seg - 1 - i
            for n, (k, g) in enumerate(chains):
                pf, hf, pb, hb = totals[n]
                a = af[s0 + k, seg_rows(g, i), :]
                a2 = ab[s0 + k, seg_rows(g, ib), :]
                totals[n] = (pf * a, a * hf + bf[s0 + k, seg_rows(g, i), :],
                             pb * a2, a2 * hb + bb[s0 + k, seg_rows(g, ib), :])

        starts = {}
        for k in range(slabs_per_pass):
            c = h0_ref[s0 + k, 0:1, :]
            for g in range(groups):
                pf, hf, _, _ = totals[k * groups + g]
                fwd = []
                for j in range(SUBLANES):
                    fwd.append(c)
                    c = hf[j:j + 1] + pf[j:j + 1] * c
                starts[(k, g, 0)] = jnp.concatenate(fwd, axis=0)
            st_ref[s0 + k, 0:1, :] = c
            c = h0_ref[s0 + k, 1:2, :]
            for g in reversed(range(groups)):
                _, _, pb, hb = totals[k * groups + g]
                bwd = [None] * SUBLANES
                for j in reversed(range(SUBLANES)):
                    bwd[j] = c
                    c = hb[j:j + 1] + pb[j:j + 1] * c
                starts[(k, g, 1)] = jnp.concatenate(bwd, axis=0)
            st_ref[s0 + k, 1:2, :] = c

        state = [(starts[(k, g, 0)], starts[(k, g, 1)]) for (k, g) in chains]
        for i in range(seg):
            ib = seg - 1 - i
            for n, (k, g) in enumerate(chains):
                hf, hb = state[n]
                hf = af[s0 + k, seg_rows(g, i), :] * hf + bf[s0 + k, seg_rows(g, i), :]
                hb = ab[s0 + k, seg_rows(g, ib), :] * hb + bb[s0 + k, seg_rows(g, ib), :]
                bf[s0 + k, seg_rows(g, i), :] = hf
                bb[s0 + k, seg_rows(g, ib), :] = hb
                state[n] = (hf, hb)
        return carry

    lax.fori_loop(0, n_slab // slabs_per_pass, scan_slabs, 0)

    def out_chunk(ci, carry):
        r0 = pl.multiple_of(ci * rc, rc)
        for s in range(n_slab):
            sl = slice(s * LANES, (s + 1) * LANES)
            hsum = bf[s, pl.ds(r0, rc), :] + bb[s, pl.ds(r0, rc), :]
            y_ref[pl.ds(r0, rc), sl] = (hsum * gx_ref[pl.ds(r0, rc), sl]).astype(BF16)
        return carry

    lax.fori_loop(0, n_chunks, out_chunk, 0)


def _lru(xr, gx, h0, l, seq_len, conv_w, conv_b, lam, wg, bg, cb, groups, cast=()):
    n = xr.shape[0]
    b = n // seq_len
    n_slab = cb // LANES
    nj = D_RNN // cb
    steps = b * nj
    cast_in = [pl.BlockSpec((None, w.shape[1] // steps, w.shape[2]), lambda bi, j: (l, bi * nj + j, 0)) for w in cast]
    cast_out = [pl.BlockSpec((w.shape[1] // steps, w.shape[2]), lambda bi, j: (bi * nj + j, 0)) for w in cast]
    seg = seq_len // (SUBLANES * groups) + SEG_PAD
    assert n_slab % min(GATE_SLABS, n_slab) == 0 and n_slab % max(1, SCAN_CHAINS // groups) == 0
    rows = SUBLANES * groups * seg
    col = lambda bi, j: (bi, j)
    par = lambda r, w: pl.BlockSpec((None, n_slab, r, w), lambda bi, j: (l, j, 0, 0))
    state = pl.BlockSpec((None, n_slab, 2, LANES), lambda bi, j: (bi, j, 0, 0))
    scan_buf = pltpu.VMEM((n_slab, rows, LANES), F32)
    return pl.pallas_call(
        functools.partial(_lru_kernel, seq_len=seq_len, seg=seg, groups=groups),
        grid=(b, D_RNN // cb),
        in_specs=[
            pl.BlockSpec((seq_len, cb), col),
            pl.BlockSpec((seq_len, cb), col),
            par(CONV_W, LANES), par(1, LANES), par(2, LANES),
            par(RNN_BLOCK_W, 4 * RNN_BLOCK_W), par(1, 4 * RNN_BLOCK_W),
            state,
        ] + cast_in,
        out_specs=[pl.BlockSpec((seq_len, cb), col), state] + cast_out,
        out_shape=[
            jax.ShapeDtypeStruct((n, D_RNN), BF16),
            jax.ShapeDtypeStruct((b, N_RNN_BLOCKS, 2, LANES), F32),
        ] + [jax.ShapeDtypeStruct(w.shape[1:], BF16) for w in cast],
        scratch_shapes=[pltpu.VMEM((n_slab, seq_len + 2 * SUBLANES, LANES), F32),
                        pltpu.VMEM((n_slab, 2, LANES), F32),
                        scan_buf, scan_buf, scan_buf, scan_buf],
        compiler_params=_params(),
        name="rglru",
    )(xr, gx, conv_w, conv_b, lam, wg, bg, h0, *cast)


def _post_kernel(x_ref, attn_ref, lru_ref, g_ref, mod_ref, nw_ref, wa_ref, wl_ref, wo_ref,
                 w1_ref, w2_ref, *rest):
    n_cast = (len(rest) - 1) // 2
    o_ref = rest[n_cast]
    for src, dst in zip(rest[:n_cast], rest[n_cast + 1:]):
        dst[...] = src[...].astype(BF16)
    half = x_ref.shape[0] // ROW_SPLIT
    for r in range(ROW_SPLIT):
        rows = slice(r * half, (r + 1) * half)
        a = _dot(attn_ref[rows, :], wa_ref[...])
        b = _dot(lru_ref[rows, :], wl_ref[...])
        merged = _sigmoid(g_ref[rows, 0:D_MODEL]) * a + _sigmoid(g_ref[rows, D_MODEL:]) * b
        x = x_ref[rows, :] + mod_ref[2:3, :] * _dot(merged.astype(BF16), wo_ref[...])
        h = (_rms(x) * nw_ref[...] * (1.0 + mod_ref[4:5, :]) + mod_ref[3:4, :]).astype(BF16)
        ff = D_MODEL
        acc = None
        for c in range(D_FF // ff):
            hid = jnp.maximum(_dot(h, w1_ref[:, c * ff:(c + 1) * ff]), 0.0)
            part = _dot((hid * hid).astype(BF16), w2_ref[c * ff:(c + 1) * ff, :])
            acc = part if acc is None else acc + part
        o_ref[rows, :] = x + mod_ref[5:6, :] * acc


def _post(x, attn, lru, gates, mod, l, cond_of_tile, norm2_w, w_attn_o, w_lru_o, w_out, w1, w2, in_place, cast=()):
    n = x.shape[0]
    tm = ROW_TILE
    steps = n // tm
    row = lambda i: (i, 0)
    cast_in = [pl.BlockSpec((None, w.shape[1] // steps, w.shape[2]), lambda i: (l + 1, i, 0)) for w in cast]
    cast_out = [pl.BlockSpec((w.shape[1] // steps, w.shape[2]), row) for w in cast]
    outs = pl.pallas_call(
        _post_kernel,
        grid=(steps,),
        in_specs=[
            pl.BlockSpec((tm, D_MODEL), row),
            pl.BlockSpec((tm, ATTN_WIDTH), row),
            pl.BlockSpec((tm, D_RNN), row),
            pl.BlockSpec((tm, 2 * D_MODEL), row),
            pl.BlockSpec((None, None, N_MOD, D_MODEL), lambda i: (l, cond_of_tile(i), 0, 0)),
            _resident((None, 1, D_MODEL), lambda i: (l, 0, 0)),
            _layer_weight(w_attn_o, l), _layer_weight(w_lru_o, l), _layer_weight(w_out, l),
            _layer_weight(w1, l), _layer_weight(w2, l),
        ] + cast_in,
        out_specs=[pl.BlockSpec((tm, D_MODEL), row)] + cast_out,
        out_shape=[jax.ShapeDtypeStruct((n, D_MODEL), F32)]
        + [jax.ShapeDtypeStruct(w.shape[1:], BF16) for w in cast],
        input_output_aliases={0: 0} if in_place else {},
        compiler_params=_params(),
        name="post",
    )(x, attn, lru, gates, mod, norm2_w, w_attn_o, w_lru_o, w_out, w1, w2, *cast)
    return outs[0], outs[1:]


def _rope_tables(n_tokens):
    rows = n_tokens // GRID_W
    row = jnp.broadcast_to(jnp.arange(rows)[:, None], (rows, GRID_W)).reshape(-1).astype(F32)
    col = jnp.broadcast_to(jnp.arange(GRID_W)[None, :], (rows, GRID_W)).reshape(-1).astype(F32)
    freqs = ROPE_THETA ** (-jnp.arange(0, ROPE_AXIS_DIM, 2, dtype=F32) / ROPE_AXIS_DIM)
    ang_r = row[:, None] * freqs
    ang_c = col[:, None] * freqs
    cr, sr, cc, sc = jnp.cos(ang_r), jnp.sin(ang_r), jnp.cos(ang_c), jnp.sin(ang_c)
    return (jnp.concatenate([cr, cr, cc, cc], axis=-1),
            jnp.concatenate([-sr, sr, -sc, sc], axis=-1))


def kernel(x_prompt, x_sample, cache_k, cache_v, state_lru, c, c_ctx, ada_w, ada_b, norm1_w, norm2_w, w_in, q_norm_w, k_norm_w, attn_sink, conv_w, conv_b, lru_lambda, lru_wa, lru_ba, lru_wi, lru_bi, w_attn_o, w_lru_o, w_out, mlp_w1, mlp_w2):
    bp, tp, _ = x_prompt.shape
    bs, ts, _ = x_sample.shape
    past = cache_k.shape[2]
    assert bs + 1 <= N_COND and ts % ROW_TILE == 0 and (bp * tp) % ROW_TILE == 0

    cond = jnp.zeros((N_COND, D_MODEL), F32).at[0].set(c_ctx).at[1:1 + bs].set(c)
    mod = _adaln(cond, ada_w, ada_b).reshape(DEPTH, N_COND, N_MOD, D_MODEL)
    tiles_per_sample = ts // ROW_TILE
    cond_prompt = lambda i: 0
    cond_sample = lambda i: 1 + i // tiles_per_sample

    w_in_b = w_in[0].astype(BF16)
    post_f32 = (w_attn_o, w_lru_o, w_out, mlp_w1, mlp_w2)
    post_w = None
    wg = jnp.concatenate([lru_wa[:, 0], lru_wi[:, 0], lru_wa[:, 1], lru_wi[:, 1]], axis=-1).astype(BF16)
    blk = lambda v: v.reshape(DEPTH, N_RNN_BLOCKS, 1, RNN_BLOCK_W)
    bg = jnp.concatenate([blk(lru_ba[:, 0]), blk(lru_bi[:, 0]), blk(lru_ba[:, 1]), blk(lru_bi[:, 1])], axis=-1)
    rope_tabs = _rope_tables(ts)

    n1 = norm1_w.reshape(DEPTH, 1, D_MODEL)
    n2 = norm2_w.reshape(DEPTH, 1, D_MODEL)
    qw = q_norm_w.reshape(DEPTH, 1, HEAD_DIM)
    kw = k_norm_w.reshape(DEPTH, 1, HEAD_DIM)
    by_block = lambda v: jnp.swapaxes(v.reshape(DEPTH, -1, N_RNN_BLOCKS, RNN_BLOCK_W), 1, 2)
    cvw = by_block(conv_w)
    cvb = by_block(conv_b)
    lam = by_block(lru_lambda)
    state_by_block = lambda v: jnp.swapaxes(v.reshape(-1, 2, N_RNN_BLOCKS, RNN_BLOCK_W), 1, 2)
    zero_state = jnp.zeros((bp, N_RNN_BLOCKS, 2, RNN_BLOCK_W), F32)

    xp = x_prompt.reshape(bp * tp, D_MODEL)
    xs = x_sample.reshape(bs * ts, D_MODEL)
    kv_cache, new_s = None, []
    for l in range(DEPTH):
        in_place = l > 0
        q, k, v, xr, gx, gates, *kv_cache = _inproj(xp, mod, l, cond_prompt, n1, w_in_b, qw, kw, kv_cache=kv_cache,
                                                    kv_seq_len=tp, tm=ROW_TILE if kv_cache else tp)
        attn = _ctx_attn(attn_sink[l], q.reshape(bp, tp, -1), k.reshape(bp, tp, -1), v.reshape(bp, tp, -1))
        lru, st, *first_w = _lru(xr, gx, zero_state, l, tp, cvw, cvb, lam, wg, bg, *LRU_CTX,
                                 cast=post_f32 if post_w is None else ())
        post_w = post_w or first_w
        more = l + 1 < DEPTH
        xp, next_w_in = _post(xp, attn.reshape(bp * tp, -1), lru, gates, mod, l, cond_prompt, n2,
                              *post_w, in_place, cast=(w_in,) if more else ())
        new_s.append(jnp.swapaxes(st, 1, 2).reshape(bp, 2, D_RNN))
        q, k, v, xr, gx, gates = _inproj(xs, mod, l, cond_sample, n1, w_in_b, qw, kw, rope_tabs=rope_tabs)
        attn = _band_attn(attn_sink[l], q.reshape(bs, ts, -1), k.reshape(bs, ts, -1),
                          v.reshape(bs, ts, -1), cache_k, cache_v, l)
        lru, _ = _lru(xr, gx, state_by_block(state_lru[:, l]), l, ts, cvw, cvb, lam, wg, bg, *LRU_LATENT)[:2]
        xs, next_post_w = _post(xs, attn.reshape(bs * ts, -1), lru, gates, mod, l, cond_sample, n2,
                                *post_w, in_place,
                                cast=post_f32 if more else ())
        if more:
            (w_in_b,), post_w = next_w_in, next_post_w

    new_k_arr, new_v_arr = kv_cache
    return (xp.reshape(bp, tp, D_MODEL), xs.reshape(bs, ts, D_MODEL), new_k_arr, new_v_arr,
            jnp.stack(new_s, axis=1))
```

```python
import functools

import jax
import jax.numpy as jnp
from jax import lax
from jax.experimental import pallas as pl
from jax.experimental.pallas import tpu as pltpu

D_MODEL = 1024
DEPTH = 4
GRID_W = 64
HEAD_DIM = 128
N_Q_HEADS = 8
N_KV_HEADS = 2
Q_PER_KV = N_Q_HEADS // N_KV_HEADS
ATTN_WIDTH = N_Q_HEADS * HEAD_DIM
KV_WIDTH = N_KV_HEADS * HEAD_DIM
WINDOW = 128
BLOCK = 128
SCALE = HEAD_DIM ** -0.5
ROPE_AXIS_DIM = HEAD_DIM // 2
ROPE_THETA = 10000.0
D_RNN = D_MODEL
N_RNN_BLOCKS = 8
RNN_BLOCK_W = D_RNN // N_RNN_BLOCKS
CONV_W = 4
CONV_LEFT = 2
LRU_C = 8.0
D_FF = 4 * D_MODEL
N_MOD = 6
EPS = 1e-6
NEG = -1e30
SPLIT_Q = ATTN_WIDTH
SPLIT_K = SPLIT_Q + KV_WIDTH
SPLIT_V = SPLIT_K + KV_WIDTH
SPLIT_XR = SPLIT_V + D_RNN
SPLIT_XG = SPLIT_XR + D_RNN
IN_WIDTH = SPLIT_XG + 2 * D_MODEL

LANES = 128
SUBLANES = 8
VMEM_LIMIT = 56 * 1024 * 1024

ROW_TILE = 512
ROW_SPLIT = 2
N_COND = 8
SEG_PAD = 4
ATTN_ROWS = 64
LOG2E = 1.4426950408889634
ADALN_COLS = 1536
CTX_SEQS = 4
LRU_CHUNK = 256
LRU_CTX = (D_RNN, 1)
LRU_LATENT = (512, 2)
SCAN_CHAINS = 8
GATE_SLABS = 4

F32 = jnp.float32
BF16 = jnp.bfloat16


def _sigmoid(x):
    return 0.5 * jnp.tanh(0.5 * x) + 0.5


def _gelu_tanh(x):
    c = 0.7978845608028654
    return 0.5 * x * (1.0 + jnp.tanh(c * (x + 0.044715 * (x * x * x))))


def _softplus(x):
    return jnp.maximum(x, 0.0) + jnp.log1p(jnp.exp(-jnp.abs(x)))


def _dot(a, b):
    return jnp.dot(a, b, preferred_element_type=F32)


def _dot_nt(a, b):
    return lax.dot_general(a, b, (((1,), (1,)), ((), ())), preferred_element_type=F32)


def _rms(x):
    return x * lax.rsqrt(jnp.mean(x * x, axis=-1, keepdims=True) + EPS)


def _resident(shape, index_map):
    return pl.BlockSpec(shape, index_map, pipeline_mode=pl.Buffered(1))


def _layer_weight(w, l):
    if w.ndim == 3:
        return _resident((None,) + w.shape[1:], lambda i: (l, 0, 0))
    return _resident(w.shape, lambda i: (0, 0))


def _params():
    return pltpu.CompilerParams(vmem_limit_bytes=VMEM_LIMIT)


def _adaln_kernel(cond_ref, w_ref, b_ref, o_ref):
    cnd = cond_ref[...]
    act = (cnd * _sigmoid(cnd)).astype(BF16)
    o_ref[...] = _dot(act, w_ref[...].astype(BF16)) + b_ref[...]


def _adaln(cond, ada_w, ada_b):
    tn = ADALN_COLS
    width = N_MOD * D_MODEL
    return pl.pallas_call(
        _adaln_kernel,
        grid=(DEPTH, width // tn),
        in_specs=[
            pl.BlockSpec((N_COND, D_MODEL), lambda l, j: (0, 0)),
            pl.BlockSpec((None, D_MODEL, tn), lambda l, j: (l, 0, j)),
            pl.BlockSpec((None, 1, tn), lambda l, j: (l, 0, j)),
        ],
        out_specs=pl.BlockSpec((None, N_COND, tn), lambda l, j: (l, 0, j)),
        out_shape=jax.ShapeDtypeStruct((DEPTH, N_COND, width), F32),
        compiler_params=_params(),
        name="adaln",
    )(cond, ada_w, ada_b.reshape(DEPTH, 1, width))


def _rope(x, cs, sn):
    lane = lax.broadcasted_iota(jnp.int32, x.shape, 1)
    half = ROPE_AXIS_DIM // 2
    swapped = jnp.where((lane & (ROPE_AXIS_DIM - 1)) < half,
                        pltpu.roll(x, HEAD_DIM - half, 1), pltpu.roll(x, half, 1))
    return x * cs + swapped * sn


def _inproj_kernel(*refs, rope, keep_kv, row_split, seq_len, slot):
    x_ref, mod_ref, nw_ref, w_ref, qw_ref, kw_ref = refs[:6]
    refs = refs[6:]
    if rope:
        cs_ref, sn_ref = refs[:2]
        refs = refs[2:]
    if keep_kv:
        refs = refs[-8:]
        k32_ref, v32_ref = refs[6:8]
    q_ref, k_ref, v_ref, xr_ref, gx_ref, g_ref = refs[:6]

    half = x_ref.shape[0] // row_split
    hs = []
    for r in range(row_split):
        rows = slice(r * half, (r + 1) * half)
        y = _rms(x_ref[rows, :]) * nw_ref[...]
        hs.append((y * (1.0 + mod_ref[1:2, :]) + mod_ref[0:1, :]).astype(BF16))
    if keep_kv and slot is not None:
        for ref in (k32_ref, v32_ref):
            for other in range(DEPTH):
                if other != slot:
                    ref[:, other] = jnp.zeros((ref.shape[0],) + ref.shape[2:], F32)
    for r in range(row_split):
        rows = slice(r * half, (r + 1) * half)
        h = hs[r]
        if keep_kv:
            kv_at = (r * half // seq_len,) + (() if slot is None else (slot,)) + (pl.ds(r * half % seq_len, half),)

        def head(z, w):
            o = _rms(z) * w
            return _rope(o, cs_ref[rows, :], sn_ref[rows, :]) if rope else o

        zq = _dot(h, w_ref[:, 0:SPLIT_Q])
        for hd in range(N_Q_HEADS):
            sl = slice(hd * HEAD_DIM, (hd + 1) * HEAD_DIM)
            q_ref[rows, sl] = head(zq[:, sl], qw_ref[...]).astype(BF16)

        zkv = _dot(h, w_ref[:, SPLIT_Q:SPLIT_V])
        for hd in range(N_KV_HEADS):
            sl = slice(hd * HEAD_DIM, (hd + 1) * HEAD_DIM)
            kn = head(zkv[:, sl], kw_ref[...])
            k_ref[rows, sl] = kn.astype(BF16)
            if keep_kv:
                k32_ref[kv_at + (hd, slice(None))] = kn
        zv = zkv[:, KV_WIDTH:2 * KV_WIDTH]
        v_ref[rows, :] = zv.astype(BF16)
        if keep_kv:
            for hd in range(N_KV_HEADS):
                v32_ref[kv_at + (hd, slice(None))] = zv[:, hd * HEAD_DIM:(hd + 1) * HEAD_DIM]

        xr_ref[rows, :] = _dot(h, w_ref[:, SPLIT_V:SPLIT_XR])
        gx_ref[rows, :] = _gelu_tanh(_dot(h, w_ref[:, SPLIT_XR:SPLIT_XG]))
        g_ref[rows, 0:D_MODEL] = _dot(h, w_ref[:, SPLIT_XG:SPLIT_XG + D_MODEL])
        g_ref[rows, D_MODEL:] = _dot(h, w_ref[:, SPLIT_XG + D_MODEL:])


def _inproj(x, mod, l, cond_of_tile, norm1_w, w_in, q_norm_w, k_norm_w, rope_tabs=None, kv_cache=None,
            kv_seq_len=None, tm=ROW_TILE):
    n = x.shape[0]
    rope = rope_tabs is not None
    keep_kv = kv_seq_len is not None
    creates = keep_kv and kv_cache is None
    row = lambda i: (i, 0)
    in_specs = [
        pl.BlockSpec((tm, D_MODEL), row),
        pl.BlockSpec((None, None, N_MOD, D_MODEL), lambda i: (l, cond_of_tile(i), 0, 0)),
        _resident((None, 1, D_MODEL), lambda i: (l, 0, 0)),
        _layer_weight(w_in, l),
        _resident((None, 1, HEAD_DIM), lambda i: (l, 0, 0)),
        _resident((None, 1, HEAD_DIM), lambda i: (l, 0, 0)),
    ]
    args = [x, mod, norm1_w, w_in, q_norm_w, k_norm_w]
    if rope:
        tiles_per_seq = rope_tabs[0].shape[0] // tm
        tab = pl.BlockSpec((tm, HEAD_DIM), lambda i: (i % tiles_per_seq, 0))
        in_specs += [tab, tab]
        args += list(rope_tabs)
    out_shape = [
        jax.ShapeDtypeStruct((n, ATTN_WIDTH), BF16),
        jax.ShapeDtypeStruct((n, KV_WIDTH), BF16),
        jax.ShapeDtypeStruct((n, KV_WIDTH), BF16),
        jax.ShapeDtypeStruct((n, D_RNN), F32),
        jax.ShapeDtypeStruct((n, D_RNN), F32),
        jax.ShapeDtypeStruct((n, 2 * D_MODEL), F32),
    ]
    out_specs = [
        pl.BlockSpec((tm, ATTN_WIDTH), row),
        pl.BlockSpec((tm, KV_WIDTH), row),
        pl.BlockSpec((tm, KV_WIDTH), row),
        pl.BlockSpec((tm, D_RNN), row),
        pl.BlockSpec((tm, D_RNN), row),
        pl.BlockSpec((tm, 2 * D_MODEL), row),
    ]
    aliases = {}
    if keep_kv:
        seqs = tm // kv_seq_len
        assert tm % kv_seq_len == 0 and n % kv_seq_len == 0
        kv_shape = (n // kv_seq_len, DEPTH, kv_seq_len, N_KV_HEADS, HEAD_DIM)
        out_shape += [jax.ShapeDtypeStruct(kv_shape, F32)] * 2
        if creates:
            kv_block = pl.BlockSpec((seqs, DEPTH, kv_seq_len, N_KV_HEADS, HEAD_DIM), lambda i: (i, 0, 0, 0, 0))
        else:
            kv_block = pl.BlockSpec((seqs, None, kv_seq_len, N_KV_HEADS, HEAD_DIM), lambda i: (i, l, 0, 0, 0))
        out_specs += [kv_block] * 2
        if kv_cache is not None:
            aliases = {len(args): 6, len(args) + 1: 7}
            in_specs += [pl.BlockSpec(memory_space=pl.ANY)] * 2
            args += list(kv_cache)
    return pl.pallas_call(
        functools.partial(_inproj_kernel, rope=rope, keep_kv=keep_kv, row_split=ROW_SPLIT,
                          seq_len=kv_seq_len, slot=l if creates else None),
        grid=(n // tm,),
        in_specs=in_specs,
        out_specs=out_specs,
        out_shape=out_shape,
        input_output_aliases=aliases,
        compiler_params=_params(),
        name="inproj_rope" if rope else "inproj",
    )(*args)


def _stack_heads(q_at, g):
    return jnp.concatenate(
        [q_at(slice((g * Q_PER_KV + j) * HEAD_DIM, (g * Q_PER_KV + j + 1) * HEAD_DIM))
         for j in range(Q_PER_KV)], axis=0)


def _sink_softmax_rows(s_scr, p_scr, den_scr, m_scr, sink_ref, head0, rows_per_head, rch, n_heads=Q_PER_KV,
                       row_sums=True):
    c1 = SCALE * LOG2E
    cols = [slice(c * LANES, (c + 1) * LANES) for c in range(s_scr.shape[1] // LANES)]
    chunks = [(pl.ds(ci * rch, rch), head0 + ci * rch // rows_per_head)
              for ci in range(n_heads * rows_per_head // rch)]
    for rws, head in chunks:
        mx = s_scr[rws, cols[0]]
        for c in cols[1:]:
            mx = jnp.maximum(mx, s_scr[rws, c])
        row_max = jnp.broadcast_to(jnp.max(mx, axis=-1, keepdims=True), mx.shape)
        m_scr[rws, :] = jnp.maximum(row_max * SCALE, sink_ref[head])
    for rws, head in chunks:
        m = m_scr[rws, :]
        mr = m * (1.0 / SCALE)
        acc = None
        for c in cols:
            e = jnp.exp2((s_scr[rws, c] - mr) * c1)
            p_scr[rws, c] = e.astype(BF16)
            if row_sums:
                acc = e if acc is None else acc + e
        den = jnp.exp(sink_ref[head] - m)
        if row_sums:
            den = jnp.broadcast_to(jnp.sum(acc, axis=-1, keepdims=True), acc.shape) + den
        den_scr[rws, :] = den


def _ctx_attn_kernel(sink_ref, q_ref, k_ref, v_ref, o_ref, s_scr, p_scr, den_scr, m_scr):
    bt, t = q_ref.shape[0], q_ref.shape[1]
    units = [(b, g) for b in range(bt) for g in range(N_KV_HEADS)]
    ones_v = jnp.ones((t, LANES), BF16)

    def scores(n, b, g):
        q4 = _stack_heads(lambda s: q_ref[b, :, s], g)
        s_scr[n % 2] = _dot_nt(q4, k_ref[b, :, g * HEAD_DIM:(g + 1) * HEAD_DIM])

    scores(0, *units[0])
    for n, (b, g) in enumerate(units):
        if n + 1 < len(units):
            scores(n + 1, *units[n + 1])
        c = n % 2
        _sink_softmax_rows(s_scr.at[c], p_scr.at[c], den_scr.at[c], m_scr.at[c], sink_ref, g * Q_PER_KV, t, ATTN_ROWS,
                           row_sums=False)
        ov = _dot(p_scr[c], jnp.concatenate([v_ref[b, :, g * HEAD_DIM:(g + 1) * HEAD_DIM], ones_v], axis=1))
        o = ov[:, 0:HEAD_DIM] / (ov[:, HEAD_DIM:] + den_scr[c])
        for j in range(Q_PER_KV):
            hs = slice((g * Q_PER_KV + j) * HEAD_DIM, (g * Q_PER_KV + j + 1) * HEAD_DIM)
            o_ref[b, :, hs] = o[j * t:(j + 1) * t].astype(BF16)


def _ctx_attn(sink, q, k, v):
    b, t = q.shape[0], q.shape[1]
    bt = CTX_SEQS
    blk = lambda w: pl.BlockSpec((bt, t, w), lambda i: (i, 0, 0))
    return pl.pallas_call(
        _ctx_attn_kernel,
        grid=(b // bt,),
        in_specs=[pl.BlockSpec(memory_space=pltpu.SMEM), blk(ATTN_WIDTH), blk(KV_WIDTH), blk(KV_WIDTH)],
        out_specs=blk(ATTN_WIDTH),
        out_shape=jax.ShapeDtypeStruct((b, t, ATTN_WIDTH), BF16),
        scratch_shapes=[pltpu.VMEM((N_KV_HEADS, Q_PER_KV * t, t), F32),
                        pltpu.VMEM((N_KV_HEADS, Q_PER_KV * t, t), BF16),
                        pltpu.VMEM((N_KV_HEADS, Q_PER_KV * t, LANES), F32),
                        pltpu.VMEM((N_KV_HEADS, Q_PER_KV * t, LANES), F32)],
        compiler_params=_params(),
        name="ctx_attn",
    )(sink, q, k, v)


def _band_attn_kernel(sink_ref, q_ref, kp_ref, kc_ref, kn_ref, vp_ref, vc_ref, vn_ref,
                      ck32_ref, cv32_ref, o_ref, ck_ref, cv_ref, s_scr, p_scr, den_scr, m_scr):
    i = pl.program_id(1)
    nb = pl.num_programs(1)
    past = ck_ref.shape[0]

    @pl.when(i == 0)
    def _():
        for g in range(N_KV_HEADS):
            sl = slice(g * HEAD_DIM, (g + 1) * HEAD_DIM)
            ck_ref[:, sl] = ck32_ref[:, g, :].astype(BF16)
            cv_ref[:, sl] = cv32_ref[:, g, :].astype(BF16)

    rows = Q_PER_KV * BLOCK

    qrow = lax.broadcasted_iota(jnp.int32, (rows, BLOCK), 0) & (BLOCK - 1)
    col = lax.broadcasted_iota(jnp.int32, (rows, BLOCK), 1)
    in_prev = col >= qrow + jnp.where(i > 0, 0, 2 * BLOCK)
    in_next = col <= qrow - jnp.where(i < nb - 1, 0, 2 * BLOCK)

    def scores(g):
        sl = slice(g * HEAD_DIM, (g + 1) * HEAD_DIM)
        q4 = _stack_heads(lambda s: q_ref[:, s], g)
        kloc = jnp.concatenate([kp_ref[:, sl], kc_ref[:, sl], kn_ref[:, sl]], axis=0)
        s_scr[g, :, 0:past] = _dot_nt(q4, ck_ref[:, sl])
        s_loc = _dot_nt(q4, kloc)
        s_scr[g, :, past:past + BLOCK] = jnp.where(in_prev, s_loc[:, 0:BLOCK], NEG)
        s_scr[g, :, past + BLOCK:past + 2 * BLOCK] = s_loc[:, BLOCK:2 * BLOCK]
        s_scr[g, :, past + 2 * BLOCK:] = jnp.where(in_next, s_loc[:, 2 * BLOCK:], NEG)

    def values(g):
        sl = slice(g * HEAD_DIM, (g + 1) * HEAD_DIM)
        vloc = jnp.concatenate([vp_ref[:, sl], vc_ref[:, sl], vn_ref[:, sl]], axis=0)
        o = (_dot(p_scr[g, :, 0:past], cv_ref[:, sl]) + _dot(p_scr[g, :, past:], vloc)) / den_scr[g]
        for j in range(Q_PER_KV):
            hs = slice((g * Q_PER_KV + j) * HEAD_DIM, (g * Q_PER_KV + j + 1) * HEAD_DIM)
            o_ref[:, hs] = o[j * BLOCK:(j + 1) * BLOCK].astype(BF16)

    for g in range(N_KV_HEADS):
        scores(g)
    for g in range(N_KV_HEADS):
        _sink_softmax_rows(s_scr.at[g], p_scr.at[g], den_scr.at[g], m_scr.at[g], sink_ref, g * Q_PER_KV, BLOCK, ATTN_ROWS)
        values(g)


def _band_attn(sink, q, k, v, ctx_k, ctx_v, l):
    b, t = q.shape[0], q.shape[1]
    nb = t // BLOCK
    past = ctx_k.shape[2]
    keys = past + 3 * BLOCK
    cur = lambda w: pl.BlockSpec((None, BLOCK, w), lambda bi, i: (bi, i, 0))
    prev = pl.BlockSpec((None, BLOCK, KV_WIDTH), lambda bi, i: (bi, jnp.maximum(i - 1, 0), 0))
    nxt = pl.BlockSpec((None, BLOCK, KV_WIDTH), lambda bi, i: (bi, jnp.minimum(i + 1, nb - 1), 0))
    ctx = pl.BlockSpec((None, None, past, N_KV_HEADS, HEAD_DIM), lambda bi, i: (bi, l, 0, 0, 0))
    return pl.pallas_call(
        _band_attn_kernel,
        grid=(b, nb),
        in_specs=[pl.BlockSpec(memory_space=pltpu.SMEM), cur(ATTN_WIDTH),
                  prev, cur(KV_WIDTH), nxt, prev, cur(KV_WIDTH), nxt, ctx, ctx],
        out_specs=cur(ATTN_WIDTH),
        out_shape=jax.ShapeDtypeStruct((b, t, ATTN_WIDTH), BF16),
        scratch_shapes=[pltpu.VMEM((past, KV_WIDTH), BF16),
                        pltpu.VMEM((past, KV_WIDTH), BF16),
                        pltpu.VMEM((N_KV_HEADS, Q_PER_KV * BLOCK, keys), F32),
                        pltpu.VMEM((N_KV_HEADS, Q_PER_KV * BLOCK, keys), BF16),
                        pltpu.VMEM((N_KV_HEADS, Q_PER_KV * BLOCK, LANES), F32),
                        pltpu.VMEM((N_KV_HEADS, Q_PER_KV * BLOCK, LANES), F32)],
        compiler_params=_params(),
        name="band_attn",
    )(sink, q, k, k, k, v, v, v, ctx_k, ctx_v)


def _lru_kernel(xr_ref, gx_ref, cw_ref, cb_ref, lam_ref, wg_ref, bg_ref, h0_ref, *rest, seq_len, seg, groups):
    n_cast = (len(rest) - 8) // 2
    y_ref, st_ref = rest[n_cast:n_cast + 2]
    xpad, ldec, af, bf, ab, bb = rest[2 * n_cast + 2:]
    for src, dst in zip(rest[:n_cast], rest[n_cast + 2:2 * n_cast + 2]):
        dst[...] = src[...].astype(BF16)
    t = seq_len
    n_slab = xr_ref.shape[1] // LANES
    rc = min(LRU_CHUNK, t)
    n_chunks = t // rc
    n_seg = SUBLANES * groups
    gate_slabs = min(GATE_SLABS, n_slab)
    rows = n_seg * seg

    for s in range(n_slab):
        xpad[s, 0:SUBLANES, :] = jnp.zeros((SUBLANES, LANES), F32)
        xpad[s, SUBLANES + t:2 * SUBLANES + t, :] = jnp.zeros((SUBLANES, LANES), F32)
        xpad[s, SUBLANES:SUBLANES + t, :] = xr_ref[:, s * LANES:(s + 1) * LANES]
        af[s, t:rows, :] = jnp.ones((rows - t, LANES), F32)
        ab[s, t:rows, :] = jnp.ones((rows - t, LANES), F32)
        bf[s, t:rows, :] = jnp.zeros((rows - t, LANES), F32)
        bb[s, t:rows, :] = jnp.zeros((rows - t, LANES), F32)

    ldec[...] = (-0.5 * LRU_C) * _softplus(-lam_ref[...])

    def gates_chunk(idx, carry):
        sp = idx // n_chunks
        r0 = pl.multiple_of((idx - sp * n_chunks) * rc, rc)
        base = r0 + SUBLANES - CONV_LEFT
        for k in range(gate_slabs):
            s = sp * gate_slabs + k
            cw = cw_ref[s]
            xc = cw[0:1, :] * xpad[s, pl.ds(base, rc, stride=1), :]
            for tap in range(1, CONV_W):
                xc = xc + cw[tap:tap + 1, :] * xpad[s, pl.ds(base + tap, rc, stride=1), :]
            xc = xc + cb_ref[s]
            half_x = 0.5 * xc
            gh = _dot(half_x.astype(BF16), wg_ref[s]) + 0.5 * bg_ref[s]
            hd = ldec[s]
            for d, (a_scr, b_scr) in enumerate(((af, bf), (ab, bb))):
                off = 2 * d * LANES
                th_r = jnp.tanh(gh[:, off:off + LANES])
                th_i = jnp.tanh(gh[:, off + LANES:off + 2 * LANES])
                log_a = hd[d:d + 1, :] * th_r + hd[d:d + 1, :]
                a = jnp.exp(log_a)
                u = jnp.tanh(log_a) * (-1.0 - a * a)
                root = jnp.where(u == 0.0, 0.0, u * lax.rsqrt(u))
                a_scr[s, pl.ds(r0, rc), :] = a
                b_scr[s, pl.ds(r0, rc), :] = root * (half_x * th_i + half_x)
        return carry

    lax.fori_loop(0, (n_slab // gate_slabs) * n_chunks, gates_chunk, 0)

    def seg_rows(g, i):
        return pl.ds(g * SUBLANES * seg + i, SUBLANES, stride=seg)

    ones = jnp.ones((SUBLANES, LANES), F32)
    zeros = jnp.zeros((SUBLANES, LANES), F32)
    slabs_per_pass = max(1, SCAN_CHAINS // groups)
    chains = [(k, g) for k in range(slabs_per_pass) for g in range(groups)]

    def scan_slabs(sg, carry):
        s0 = sg * slabs_per_pass

        totals = [(ones, zeros, ones, zeros) for _ in chains]
        for i in range(seg):
            ib = seg - 1 - i
            for n, (k, g) in enumerate(chains):
                pf, hf, pb, hb = totals[n]
                a = af[s0 + k, seg_rows(g, i), :]
                a2 = ab[s0 + k, seg_rows(g, ib), :]
                totals[n] = (pf * a, a * hf + bf[s0 + k, seg_rows(g, i), :],
                             pb * a2, a2 * hb + bb[s0 + k, seg_rows(g, ib), :])

        starts = {}
        for k in range(slabs_per_pass):
            c = h0_ref[s0 + k, 0:1, :]
            for g in range(groups):
                pf, hf, _, _ = totals[k * groups + g]
                fwd = []
                for j in range(SUBLANES):
                    fwd.append(c)
                    c = hf[j:j + 1] + pf[j:j + 1] * c
                starts[(k, g, 0)] = jnp.concatenate(fwd, axis=0)
            st_ref[s0 + k, 0:1, :] = c
            c = h0_ref[s0 + k, 1:2, :]
            for g in reversed(range(groups)):
                _, _, pb, hb = totals[k * groups + g]
                bwd = [None] * SUBLANES
                for j in reversed(range(SUBLANES)):
                    bwd[j] = c
                    c = hb[j:j + 1] + pb[j:j + 1] * c
                starts[(k, g, 1)] = jnp.concatenate(bwd, axis=0)
            st_ref[s0 + k, 1:2, :] = c

        state = [(starts[(k, g, 0)], starts[(k, g, 1)]) for (k, g) in chains]
        for i in range(seg):
            ib = seg - 1 - i
            for n, (k, g) in enumerate(chains):
                hf, hb = state[n]
                hf = af[s0 + k, seg_rows(g, i), :] * hf + bf[s0 + k, seg_rows(g, i), :]
                hb = ab[s0 + k, seg_rows(g, ib), :] * hb + bb[s0 + k, seg_rows(g, ib), :]
                bf[s0 + k, seg_rows(g, i), :] = hf
                bb[s0 + k, seg_rows(g, ib), :] = hb
                state[n] = (hf, hb)
        return carry

    lax.fori_loop(0, n_slab // slabs_per_pass, scan_slabs, 0)

    def out_chunk(ci, carry):
        r0 = pl.multiple_of(ci * rc, rc)
        for s in range(n_slab):
            sl = slice(s * LANES, (s + 1) * LANES)
            hsum = bf[s, pl.ds(r0, rc), :] + bb[s, pl.ds(r0, rc), :]
            y_ref[pl.ds(r0, rc), sl] = (hsum * gx_ref[pl.ds(r0, rc), sl]).astype(BF16)
        return carry

    lax.fori_loop(0, n_chunks, out_chunk, 0)


def _lru(xr, gx, h0, l, seq_len, conv_w, conv_b, lam, wg, bg, cb, groups, cast=()):
    n = xr.shape[0]
    b = n // seq_len
    n_slab = cb // LANES
    nj = D_RNN // cb
    steps = b * nj
    cast_in = [pl.BlockSpec((None, w.shape[1] // steps, w.shape[2]), lambda bi, j: (l, bi * nj + j, 0)) for w in cast]
    cast_out = [pl.BlockSpec((w.shape[1] // steps, w.shape[2]), lambda bi, j: (bi * nj + j, 0)) for w in cast]
    seg = seq_len // (SUBLANES * groups) + SEG_PAD
    assert n_slab % min(GATE_SLABS, n_slab) == 0 and n_slab % max(1, SCAN_CHAINS // groups) == 0
    rows = SUBLANES * groups * seg
    col = lambda bi, j: (bi, j)
    par = lambda r, w: pl.BlockSpec((None, n_slab, r, w), lambda bi, j: (l, j, 0, 0))
    state = pl.BlockSpec((None, n_slab, 2, LANES), lambda bi, j: (bi, j, 0, 0))
    scan_buf = pltpu.VMEM((n_slab, rows, LANES), F32)
    return pl.pallas_call(
        functools.partial(_lru_kernel, seq_len=seq_len, seg=seg, groups=groups),
        grid=(b, D_RNN // cb),
        in_specs=[
            pl.BlockSpec((seq_len, cb), col),
            pl.BlockSpec((seq_len, cb), col),
            par(CONV_W, LANES), par(1, LANES), par(2, LANES),
            par(RNN_BLOCK_W, 4 * RNN_BLOCK_W), par(1, 4 * RNN_BLOCK_W),
            state,
        ] + cast_in,
        out_specs=[pl.BlockSpec((seq_len, cb), col), state] + cast_out,
        out_shape=[
            jax.ShapeDtypeStruct((n, D_RNN), BF16),
            jax.ShapeDtypeStruct((b, N_RNN_BLOCKS, 2, LANES), F32),
        ] + [jax.ShapeDtypeStruct(w.shape[1:], BF16) for w in cast],
        scratch_shapes=[pltpu.VMEM((n_slab, seq_len + 2 * SUBLANES, LANES), F32),
                        pltpu.VMEM((n_slab, 2, LANES), F32),
                        scan_buf, scan_buf, scan_buf, scan_buf],
        compiler_params=_params(),
        name="rglru",
    )(xr, gx, conv_w, conv_b, lam, wg, bg, h0, *cast)


def _post_kernel(x_ref, attn_ref, lru_ref, g_ref, mod_ref, nw_ref, wa_ref, wl_ref, wo_ref,
                 w1_ref, w2_ref, *rest):
    n_cast = (len(rest) - 1) // 2
    o_ref = rest[n_cast]
    for src, dst in zip(rest[:n_cast], rest[n_cast + 1:]):
        dst[...] = src[...].astype(BF16)
    half = x_ref.shape[0] // ROW_SPLIT
    for r in range(ROW_SPLIT):
        rows = slice(r * half, (r + 1) * half)
        a = _dot(attn_ref[rows, :], wa_ref[...])
        b = _dot(lru_ref[rows, :], wl_ref[...])
        merged = _sigmoid(g_ref[rows, 0:D_MODEL]) * a + _sigmoid(g_ref[rows, D_MODEL:]) * b
        x = x_ref[rows, :] + mod_ref[2:3, :] * _dot(merged.astype(BF16), wo_ref[...])
        h = (_rms(x) * nw_ref[...] * (1.0 + mod_ref[4:5, :]) + mod_ref[3:4, :]).astype(BF16)
        ff = D_MODEL
        acc = None
        for c in range(D_FF // ff):
            hid = jnp.maximum(_dot(h, w1_ref[:, c * ff:(c + 1) * ff]), 0.0)
            part = _dot((hid * hid).astype(BF16), w2_ref[c * ff:(c + 1) * ff, :])
            acc = part if acc is None else acc + part
        o_ref[rows, :] = x + mod_ref[5:6, :] * acc


def _post(x, attn, lru, gates, mod, l, cond_of_tile, norm2_w, w_attn_o, w_lru_o, w_out, w1, w2, in_place, cast=()):
    n = x.shape[0]
    tm = ROW_TILE
    steps = n // tm
    row = lambda i: (i, 0)
    cast_in = [pl.BlockSpec((None, w.shape[1] // steps, w.shape[2]), lambda i: (l + 1, i, 0)) for w in cast]
    cast_out = [pl.BlockSpec((w.shape[1] // steps, w.shape[2]), row) for w in cast]
    outs = pl.pallas_call(
        _post_kernel,
        grid=(steps,),
        in_specs=[
            pl.BlockSpec((tm, D_MODEL), row),
            pl.BlockSpec((tm, ATTN_WIDTH), row),
            pl.BlockSpec((tm, D_RNN), row),
            pl.BlockSpec((tm, 2 * D_MODEL), row),
            pl.BlockSpec((None, None, N_MOD, D_MODEL), lambda i: (l, cond_of_tile(i), 0, 0)),
            _resident((None, 1, D_MODEL), lambda i: (l, 0, 0)),
            _layer_weight(w_attn_o, l), _layer_weight(w_lru_o, l), _layer_weight(w_out, l),
            _layer_weight(w1, l), _layer_weight(w2, l),
        ] + cast_in,
        out_specs=[pl.BlockSpec((tm, D_MODEL), row)] + cast_out,
        out_shape=[jax.ShapeDtypeStruct((n, D_MODEL), F32)]
        + [jax.ShapeDtypeStruct(w.shape[1:], BF16) for w in cast],
        input_output_aliases={0: 0} if in_place else {},
        compiler_params=_params(),
        name="post",
    )(x, attn, lru, gates, mod, norm2_w, w_attn_o, w_lru_o, w_out, w1, w2, *cast)
    return outs[0], outs[1:]


def _rope_tables(n_tokens):
    rows = n_tokens // GRID_W
    row = jnp.broadcast_to(jnp.arange(rows)[:, None], (rows, GRID_W)).reshape(-1).astype(F32)
    col = jnp.broadcast_to(jnp.arange(GRID_W)[None, :], (rows, GRID_W)).reshape(-1).astype(F32)
    freqs = ROPE_THETA ** (-jnp.arange(0, ROPE_AXIS_DIM, 2, dtype=F32) / ROPE_AXIS_DIM)
    ang_r = row[:, None] * freqs
    ang_c = col[:, None] * freqs
    cr, sr, cc, sc = jnp.cos(ang_r), jnp.sin(ang_r), jnp.cos(ang_c), jnp.sin(ang_c)
    return (jnp.concatenate([cr, cr, cc, cc], axis=-1),
            jnp.concatenate([-sr, sr, -sc, sc], axis=-1))


def kernel(x_prompt, x_sample, cache_k, cache_v, state_lru, c, c_ctx, ada_w, ada_b, norm1_w, norm2_w, w_in, q_norm_w, k_norm_w, attn_sink, conv_w, conv_b, lru_lambda, lru_wa, lru_ba, lru_wi, lru_bi, w_attn_o, w_lru_o, w_out, mlp_w1, mlp_w2):
    bp, tp, _ = x_prompt.shape
    bs, ts, _ = x_sample.shape
    past = cache_k.shape[2]
    assert bs + 1 <= N_COND and ts % ROW_TILE == 0 and (bp * tp) % ROW_TILE == 0

    cond = jnp.zeros((N_COND, D_MODEL), F32).at[0].set(c_ctx).at[1:1 + bs].set(c)
    mod = _adaln(cond, ada_w, ada_b).reshape(DEPTH, N_COND, N_MOD, D_MODEL)
    tiles_per_sample = ts // ROW_TILE
    cond_prompt = lambda i: 0
    cond_sample = lambda i: 1 + i // tiles_per_sample

    w_in_b = w_in[0].astype(BF16)
    post_f32 = (w_attn_o, w_lru_o, w_out, mlp_w1, mlp_w2)
    post_w = None
    wg = jnp.concatenate([lru_wa[:, 0], lru_wi[:, 0], lru_wa[:, 1], lru_wi[:, 1]], axis=-1).astype(BF16)
    blk = lambda v: v.reshape(DEPTH, N_RNN_BLOCKS, 1, RNN_BLOCK_W)
    bg = jnp.concatenate([blk(lru_ba[:, 0]), blk(lru_bi[:, 0]), blk(lru_ba[:, 1]), blk(lru_bi[:, 1])], axis=-1)
    rope_tabs = _rope_tables(ts)

    n1 = norm1_w.reshape(DEPTH, 1, D_MODEL)
    n2 = norm2_w.reshape(DEPTH, 1, D_MODEL)
    qw = q_norm_w.reshape(DEPTH, 1, HEAD_DIM)
    kw = k_norm_w.reshape(DEPTH, 1, HEAD_DIM)
    by_block = lambda v: jnp.swapaxes(v.reshape(DEPTH, -1, N_RNN_BLOCKS, RNN_BLOCK_W), 1, 2)
    cvw = by_block(conv_w)
    cvb = by_block(conv_b)
    lam = by_block(lru_lambda)
    state_by_block = lambda v: jnp.swapaxes(v.reshape(-1, 2, N_RNN_BLOCKS, RNN_BLOCK_W), 1, 2)
    zero_state = jnp.zeros((bp, N_RNN_BLOCKS, 2, RNN_BLOCK_W), F32)

    xp = x_prompt.reshape(bp * tp, D_MODEL)
    xs = x_sample.reshape(bs * ts, D_MODEL)
    kv_cache, new_s = None, []
    for l in range(DEPTH):
        in_place = l > 0
        q, k, v, xr, gx, gates, *kv_cache = _inproj(xp, mod, l, cond_prompt, n1, w_in_b, qw, kw, kv_cache=kv_cache,
                                                    kv_seq_len=tp, tm=ROW_TILE if kv_cache else tp)
        attn = _ctx_attn(attn_sink[l], q.reshape(bp, tp, -1), k.reshape(bp, tp, -1), v.reshape(bp, tp, -1))
        lru, st, *first_w = _lru(xr, gx, zero_state, l, tp, cvw, cvb, lam, wg, bg, *LRU_CTX,
                                 cast=post_f32 if post_w is None else ())
        post_w = post_w or first_w
        more = l + 1 < DEPTH
        xp, next_w_in = _post(xp, attn.reshape(bp * tp, -1), lru, gates, mod, l, cond_prompt, n2,
                              *post_w, in_place, cast=(w_in,) if more else ())
        new_s.append(jnp.swapaxes(st, 1, 2).reshape(bp, 2, D_RNN))
        q, k, v, xr, gx, gates = _inproj(xs, mod, l, cond_sample, n1, w_in_b, qw, kw, rope_tabs=rope_tabs)
        attn = _band_attn(attn_sink[l], q.reshape(bs, ts, -1), k.reshape(bs, ts, -1),
                          v.reshape(bs, ts, -1), cache_k, cache_v, l)
        lru, _ = _lru(xr, gx, state_by_block(state_lru[:, l]), l, ts, cvw, cvb, lam, wg, bg, *LRU_LATENT)[:2]
        xs, next_post_w = _post(xs, attn.reshape(bs * ts, -1), lru, gates, mod, l, cond_sample, n2,
                                *post_w, in_place,
                                cast=post_f32 if more else ())
        if more:
            (w_in_b,), post_w = next_w_in, next_post_w

    new_k_arr, new_v_arr = kv_cache
    return (xp.reshape(bp, tp, D_MODEL), xs.reshape(bs, ts, D_MODEL), new_k_arr, new_v_arr,
            jnp.stack(new_s, axis=1))
```

```python
import functools

import jax
import jax.numpy as jnp
from jax import lax
from jax.experimental import pallas as pl
from jax.experimental.pallas import tpu as pltpu

D_MODEL = 1024
DEPTH = 4
GRID_W = 64
HEAD_DIM = 128
N_Q_HEADS = 8
N_KV_HEADS = 2
Q_PER_KV = N_Q_HEADS // N_KV_HEADS
ATTN_WIDTH = N_Q_HEADS * HEAD_DIM
KV_WIDTH = N_KV_HEADS * HEAD_DIM
WINDOW = 128
BLOCK = 128
SCALE = HEAD_DIM ** -0.5
ROPE_AXIS_DIM = HEAD_DIM // 2
ROPE_THETA = 10000.0
D_RNN = D_MODEL
N_RNN_BLOCKS = 8
RNN_BLOCK_W = D_RNN // N_RNN_BLOCKS
CONV_W = 4
CONV_LEFT = 2
LRU_C = 8.0
D_FF = 4 * D_MODEL
N_MOD = 6
EPS = 1e-6
NEG = -1e30
SPLIT_Q = ATTN_WIDTH
SPLIT_K = SPLIT_Q + KV_WIDTH
SPLIT_V = SPLIT_K + KV_WIDTH
SPLIT_XR = SPLIT_V + D_RNN
SPLIT_XG = SPLIT_XR + D_RNN
IN_WIDTH = SPLIT_XG + 2 * D_MODEL

LANES = 128
SUBLANES = 8
VMEM_LIMIT = 56 * 1024 * 1024

ROW_TILE = 512
ROW_SPLIT = 2
N_COND = 8
SEG_PAD = 4
ATTN_ROWS = 64
LOG2E = 1.4426950408889634
ADALN_COLS = 1536
CTX_SEQS = 4
LRU_CHUNK = 256
LRU_CTX = (D_RNN, 1)
LRU_LATENT = (512, 2)
SCAN_CHAINS = 8
GATE_SLABS = 4

F32 = jnp.float32
BF16 = jnp.bfloat16


def _sigmoid(x):
    return 0.5 * jnp.tanh(0.5 * x) + 0.5


def _gelu_tanh(x):
    c = 0.7978845608028654
    return 0.5 * x * (1.0 + jnp.tanh(c * (x + 0.044715 * (x * x * x))))


def _softplus(x):
    return jnp.maximum(x, 0.0) + jnp.log1p(jnp.exp(-jnp.abs(x)))


def _dot(a, b):
    return jnp.dot(a, b, preferred_element_type=F32)


def _dot_nt(a, b):
    return lax.dot_general(a, b, (((1,), (1,)), ((), ())), preferred_element_type=F32)


def _rms(x):
    return x * lax.rsqrt(jnp.mean(x * x, axis=-1, keepdims=True) + EPS)


def _resident(shape, index_map):
    return pl.BlockSpec(shape, index_map, pipeline_mode=pl.Buffered(1))


def _layer_weight(w, l):
    if w.ndim == 3:
        return _resident((None,) + w.shape[1:], lambda i: (l, 0, 0))
    return _resident(w.shape, lambda i: (0, 0))


def _params():
    return pltpu.CompilerParams(vmem_limit_bytes=VMEM_LIMIT)


def _adaln_kernel(cond_ref, w_ref, b_ref, o_ref):
    cnd = cond_ref[...]
    act = (cnd * _sigmoid(cnd)).astype(BF16)
    o_ref[...] = _dot(act, w_ref[...].astype(BF16)) + b_ref[...]


def _adaln(cond, ada_w, ada_b):
    tn = ADALN_COLS
    width = N_MOD * D_MODEL
    return pl.pallas_call(
        _adaln_kernel,
        grid=(DEPTH, width // tn),
        in_specs=[
            pl.BlockSpec((N_COND, D_MODEL), lambda l, j: (0, 0)),
            pl.BlockSpec((None, D_MODEL, tn), lambda l, j: (l, 0, j)),
            pl.BlockSpec((None, 1, tn), lambda l, j: (l, 0, j)),
        ],
        out_specs=pl.BlockSpec((None, N_COND, tn), lambda l, j: (l, 0, j)),
        out_shape=jax.ShapeDtypeStruct((DEPTH, N_COND, width), F32),
        compiler_params=_params(),
        name="adaln",
    )(cond, ada_w, ada_b.reshape(DEPTH, 1, width))


def _rope(x, cs, sn):
    lane = lax.broadcasted_iota(jnp.int32, x.shape, 1)
    half = ROPE_AXIS_DIM // 2
    swapped = jnp.where((lane & (ROPE_AXIS_DIM - 1)) < half,
                        pltpu.roll(x, HEAD_DIM - half, 1), pltpu.roll(x, half, 1))
    return x * cs + swapped * sn


def _inproj_kernel(*refs, rope, keep_kv, row_split, seq_len, slot):
    x_ref, mod_ref, nw_ref, w_ref, qw_ref, kw_ref = refs[:6]
    refs = refs[6:]
    if rope:
        cs_ref, sn_ref = refs[:2]
        refs = refs[2:]
    if keep_kv:
        refs = refs[-8:]
        k32_ref, v32_ref = refs[6:8]
    q_ref, k_ref, v_ref, xr_ref, gx_ref, g_ref = refs[:6]

    half = x_ref.shape[0] // row_split
    hs = []
    for r in range(row_split):
        rows = slice(r * half, (r + 1) * half)
        y = _rms(x_ref[rows, :]) * nw_ref[...]
        hs.append((y * (1.0 + mod_ref[1:2, :]) + mod_ref[0:1, :]).astype(BF16))
    if keep_kv and slot is not None:
        for ref in (k32_ref, v32_ref):
            for other in range(DEPTH):
                if other != slot:
                    ref[:, other] = jnp.zeros((ref.shape[0],) + ref.shape[2:], F32)
    for r in range(row_split):
        rows = slice(r * half, (r + 1) * half)
        h = hs[r]
        if keep_kv:
            kv_at = (r * half // seq_len,) + (() if slot is None else (slot,)) + (pl.ds(r * half % seq_len, half),)

        def head(z, w):
            o = _rms(z) * w
            return _rope(o, cs_ref[rows, :], sn_ref[rows, :]) if rope else o

        zq = _dot(h, w_ref[:, 0:SPLIT_Q])
        for hd in range(N_Q_HEADS):
            sl = slice(hd * HEAD_DIM, (hd + 1) * HEAD_DIM)
            q_ref[rows, sl] = head(zq[:, sl], qw_ref[...]).astype(BF16)

        zkv = _dot(h, w_ref[:, SPLIT_Q:SPLIT_V])
        for hd in range(N_KV_HEADS):
            sl = slice(hd * HEAD_DIM, (hd + 1) * HEAD_DIM)
            kn = head(zkv[:, sl], kw_ref[...])
            k_ref[rows, sl] = kn.astype(BF16)
            if keep_kv:
                k32_ref[kv_at + (hd, slice(None))] = kn
        zv = zkv[:, KV_WIDTH:2 * KV_WIDTH]
        v_ref[rows, :] = zv.astype(BF16)
        if keep_kv:
            for hd in range(N_KV_HEADS):
                v32_ref[kv_at + (hd, slice(None))] = zv[:, hd * HEAD_DIM:(hd + 1) * HEAD_DIM]

        xr_ref[rows, :] = _dot(h, w_ref[:, SPLIT_V:SPLIT_XR])
        gx_ref[rows, :] = _gelu_tanh(_dot(h, w_ref[:, SPLIT_XR:SPLIT_XG]))
        g_ref[rows, 0:D_MODEL] = _dot(h, w_ref[:, SPLIT_XG:SPLIT_XG + D_MODEL])
        g_ref[rows, D_MODEL:] = _dot(h, w_ref[:, SPLIT_XG + D_MODEL:])


def _inproj(x, mod, l, cond_of_tile, norm1_w, w_in, q_norm_w, k_norm_w, rope_tabs=None, kv_cache=None,
            kv_seq_len=None, tm=ROW_TILE):
    n = x.shape[0]
    rope = rope_tabs is not None
    keep_kv = kv_seq_len is not None
    creates = keep_kv and kv_cache is None
    row = lambda i: (i, 0)
    in_specs = [
        pl.BlockSpec((tm, D_MODEL), row),
        pl.BlockSpec((None, None, N_MOD, D_MODEL), lambda i: (l, cond_of_tile(i), 0, 0)),
        _resident((None, 1, D_MODEL), lambda i: (l, 0, 0)),
        _layer_weight(w_in, l),
        _resident((None, 1, HEAD_DIM), lambda i: (l, 0, 0)),
        _resident((None, 1, HEAD_DIM), lambda i: (l, 0, 0)),
    ]
    args = [x, mod, norm1_w, w_in, q_norm_w, k_norm_w]
    if rope:
        tiles_per_seq = rope_tabs[0].shape[0] // tm
        tab = pl.BlockSpec((tm, HEAD_DIM), lambda i: (i % tiles_per_seq, 0))
        in_specs += [tab, tab]
        args += list(rope_tabs)
    out_shape = [
        jax.ShapeDtypeStruct((n, ATTN_WIDTH), BF16),
        jax.ShapeDtypeStruct((n, KV_WIDTH), BF16),
        jax.ShapeDtypeStruct((n, KV_WIDTH), BF16),
        jax.ShapeDtypeStruct((n, D_RNN), F32),
        jax.ShapeDtypeStruct((n, D_RNN), F32),
        jax.ShapeDtypeStruct((n, 2 * D_MODEL), F32),
    ]
    out_specs = [
        pl.BlockSpec((tm, ATTN_WIDTH), row),
        pl.BlockSpec((tm, KV_WIDTH), row),
        pl.BlockSpec((tm, KV_WIDTH), row),
        pl.BlockSpec((tm, D_RNN), row),
        pl.BlockSpec((tm, D_RNN), row),
        pl.BlockSpec((tm, 2 * D_MODEL), row),
    ]
    aliases = {}
    if keep_kv:
        seqs = tm // kv_seq_len
        assert tm % kv_seq_len == 0 and n % kv_seq_len == 0
        kv_shape = (n // kv_seq_len, DEPTH, kv_seq_len, N_KV_HEADS, HEAD_DIM)
        out_shape += [jax.ShapeDtypeStruct(kv_shape, F32)] * 2
        if creates:
            kv_block = pl.BlockSpec((seqs, DEPTH, kv_seq_len, N_KV_HEADS, HEAD_DIM), lambda i: (i, 0, 0, 0, 0))
        else:
            kv_block = pl.BlockSpec((seqs, None, kv_seq_len, N_KV_HEADS, HEAD_DIM), lambda i: (i, l, 0, 0, 0))
        out_specs += [kv_block] * 2
        if kv_cache is not None:
            aliases = {len(args): 6, len(args) + 1: 7}
            in_specs += [pl.BlockSpec(memory_space=pl.ANY)] * 2
            args += list(kv_cache)
    return pl.pallas_call(
        functools.partial(_inproj_kernel, rope=rope, keep_kv=keep_kv, row_split=ROW_SPLIT,
                          seq_len=kv_seq_len, slot=l if creates else None),
        grid=(n // tm,),
        in_specs=in_specs,
        out_specs=out_specs,
        out_shape=out_shape,
        input_output_aliases=aliases,
        compiler_params=_params(),
        name="inproj_rope" if rope else "inproj",
    )(*args)


def _stack_heads(q_at, g):
    return jnp.concatenate(
        [q_at(slice((g * Q_PER_KV + j) * HEAD_DIM, (g * Q_PER_KV + j + 1) * HEAD_DIM))
         for j in range(Q_PER_KV)], axis=0)


def _sink_softmax_rows(s_scr, p_scr, den_scr, m_scr, sink_ref, head0, rows_per_head, rch, n_heads=Q_PER_KV,
                       row_sums=True):
    c1 = SCALE * LOG2E
    cols = [slice(c * LANES, (c + 1) * LANES) for c in range(s_scr.shape[1] // LANES)]
    chunks = [(pl.ds(ci * rch, rch), head0 + ci * rch // rows_per_head)
              for ci in range(n_heads * rows_per_head // rch)]
    for rws, head in chunks:
        mx = s_scr[rws, cols[0]]
        for c in cols[1:]:
            mx = jnp.maximum(mx, s_scr[rws, c])
        row_max = jnp.broadcast_to(jnp.max(mx, axis=-1, keepdims=True), mx.shape)
        m_scr[rws, :] = jnp.maximum(row_max * SCALE, sink_ref[head])
    for rws, head in chunks:
        m = m_scr[rws, :]
        mr = m * (1.0 / SCALE)
        acc = None
        for c in cols:
            e = jnp.exp2((s_scr[rws, c] - mr) * c1)
            p_scr[rws, c] = e.astype(BF16)
            if row_sums:
                acc = e if acc is None else acc + e
        den = jnp.exp(sink_ref[head] - m)
        if row_sums:
            den = jnp.broadcast_to(jnp.sum(acc, axis=-1, keepdims=True), acc.shape) + den
        den_scr[rws, :] = den


def _ctx_attn_kernel(sink_ref, q_ref, k_ref, v_ref, o_ref, s_scr, p_scr, den_scr, m_scr):
    bt, t = q_ref.shape[0], q_ref.shape[1]
    units = [(b, g) for b in range(bt) for g in range(N_KV_HEADS)]
    ones_v = jnp.ones((t, LANES), BF16)

    def scores(n, b, g):
        q4 = _stack_heads(lambda s: q_ref[b, :, s], g)
        s_scr[n % 2] = _dot_nt(q4, k_ref[b, :, g * HEAD_DIM:(g + 1) * HEAD_DIM])

    scores(0, *units[0])
    for n, (b, g) in enumerate(units):
        if n + 1 < len(units):
            scores(n + 1, *units[n + 1])
        c = n % 2
        _sink_softmax_rows(s_scr.at[c], p_scr.at[c], den_scr.at[c], m_scr.at[c], sink_ref, g * Q_PER_KV, t, ATTN_ROWS,
                           row_sums=False)
        ov = _dot(p_scr[c], jnp.concatenate([v_ref[b, :, g * HEAD_DIM:(g + 1) * HEAD_DIM], ones_v], axis=1))
        o = ov[:, 0:HEAD_DIM] / (ov[:, HEAD_DIM:] + den_scr[c])
        for j in range(Q_PER_KV):
            hs = slice((g * Q_PER_KV + j) * HEAD_DIM, (g * Q_PER_KV + j + 1) * HEAD_DIM)
            o_ref[b, :, hs] = o[j * t:(j + 1) * t].astype(BF16)


def _ctx_attn(sink, q, k, v):
    b, t = q.shape[0], q.shape[1]
    bt = CTX_SEQS
    blk = lambda w: pl.BlockSpec((bt, t, w), lambda i: (i, 0, 0))
    return pl.pallas_call(
        _ctx_attn_kernel,
        grid=(b // bt,),
        in_specs=[pl.BlockSpec(memory_space=pltpu.SMEM), blk(ATTN_WIDTH), blk(KV_WIDTH), blk(KV_WIDTH)],
        out_specs=blk(ATTN_WIDTH),
        out_shape=jax.ShapeDtypeStruct((b, t, ATTN_WIDTH), BF16),
        scratch_shapes=[pltpu.VMEM((N_KV_HEADS, Q_PER_KV * t, t), F32),
                        pltpu.VMEM((N_KV_HEADS, Q_PER_KV * t, t), BF16),
                        pltpu.VMEM((N_KV_HEADS, Q_PER_KV * t, LANES), F32),
                        pltpu.VMEM((N_KV_HEADS, Q_PER_KV * t, LANES), F32)],
        compiler_params=_params(),
        name="ctx_attn",
    )(sink, q, k, v)


def _band_attn_kernel(sink_ref, q_ref, kp_ref, kc_ref, kn_ref, vp_ref, vc_ref, vn_ref,
                      ck32_ref, cv32_ref, o_ref, ck_ref, cv_ref, s_scr, p_scr, den_scr, m_scr):
    i = pl.program_id(1)
    nb = pl.num_programs(1)
    past = ck_ref.shape[0]

    @pl.when(i == 0)
    def _():
        for g in range(N_KV_HEADS):
            sl = slice(g * HEAD_DIM, (g + 1) * HEAD_DIM)
            ck_ref[:, sl] = ck32_ref[:, g, :].astype(BF16)
            cv_ref[:, sl] = cv32_ref[:, g, :].astype(BF16)

    rows = Q_PER_KV * BLOCK

    qrow = lax.broadcasted_iota(jnp.int32, (rows, BLOCK), 0) & (BLOCK - 1)
    col = lax.broadcasted_iota(jnp.int32, (rows, BLOCK), 1)
    in_prev = col >= qrow + jnp.where(i > 0, 0, 2 * BLOCK)
    in_next = col <= qrow - jnp.where(i < nb - 1, 0, 2 * BLOCK)

    def scores(g):
        sl = slice(g * HEAD_DIM, (g + 1) * HEAD_DIM)
        q4 = _stack_heads(lambda s: q_ref[:, s], g)
        kloc = jnp.concatenate([kp_ref[:, sl], kc_ref[:, sl], kn_ref[:, sl]], axis=0)
        s_scr[g, :, 0:past] = _dot_nt(q4, ck_ref[:, sl])
        s_loc = _dot_nt(q4, kloc)
        s_scr[g, :, past:past + BLOCK] = jnp.where(in_prev, s_loc[:, 0:BLOCK], NEG)
        s_scr[g, :, past + BLOCK:past + 2 * BLOCK] = s_loc[:, BLOCK:2 * BLOCK]
        s_scr[g, :, past + 2 * BLOCK:] = jnp.where(in_next, s_loc[:, 2 * BLOCK:], NEG)

    def values(g):
        sl = slice(g * HEAD_DIM, (g + 1) * HEAD_DIM)
        vloc = jnp.concatenate([vp_ref[:, sl], vc_ref[:, sl], vn_ref[:, sl]], axis=0)
        ov = (_dot(p_scr[g, :, 0:past], jnp.concatenate([cv_ref[:, sl], jnp.ones((past, LANES), BF16)], axis=1))
              + _dot(p_scr[g, :, past:], jnp.concatenate([vloc, jnp.ones((3 * BLOCK, LANES), BF16)], axis=1)))
        o = ov[:, 0:HEAD_DIM] / (ov[:, HEAD_DIM:] + den_scr[g])
        for j in range(Q_PER_KV):
            hs = slice((g * Q_PER_KV + j) * HEAD_DIM, (g * Q_PER_KV + j + 1) * HEAD_DIM)
            o_ref[:, hs] = o[j * BLOCK:(j + 1) * BLOCK].astype(BF16)

    for g in range(N_KV_HEADS):
        scores(g)
    for g in range(N_KV_HEADS):
        _sink_softmax_rows(s_scr.at[g], p_scr.at[g], den_scr.at[g], m_scr.at[g], sink_ref, g * Q_PER_KV, BLOCK, ATTN_ROWS,
                           row_sums=False)
        values(g)


def _band_attn(sink, q, k, v, ctx_k, ctx_v, l):
    b, t = q.shape[0], q.shape[1]
    nb = t // BLOCK
    past = ctx_k.shape[2]
    keys = past + 3 * BLOCK
    cur = lambda w: pl.BlockSpec((None, BLOCK, w), lambda bi, i: (bi, i, 0))
    prev = pl.BlockSpec((None, BLOCK, KV_WIDTH), lambda bi, i: (bi, jnp.maximum(i - 1, 0), 0))
    nxt = pl.BlockSpec((None, BLOCK, KV_WIDTH), lambda bi, i: (bi, jnp.minimum(i + 1, nb - 1), 0))
    ctx = pl.BlockSpec((None, None, past, N_KV_HEADS, HEAD_DIM), lambda bi, i: (bi, l, 0, 0, 0))
    return pl.pallas_call(
        _band_attn_kernel,
        grid=(b, nb),
        in_specs=[pl.BlockSpec(memory_space=pltpu.SMEM), cur(ATTN_WIDTH),
                  prev, cur(KV_WIDTH), nxt, prev, cur(KV_WIDTH), nxt, ctx, ctx],
        out_specs=cur(ATTN_WIDTH),
        out_shape=jax.ShapeDtypeStruct((b, t, ATTN_WIDTH), BF16),
        scratch_shapes=[pltpu.VMEM((past, KV_WIDTH), BF16),
                        pltpu.VMEM((past, KV_WIDTH), BF16),
                        pltpu.VMEM((N_KV_HEADS, Q_PER_KV * BLOCK, keys), F32),
                        pltpu.VMEM((N_KV_HEADS, Q_PER_KV * BLOCK, keys), BF16),
                        pltpu.VMEM((N_KV_HEADS, Q_PER_KV * BLOCK, LANES), F32),
                        pltpu.VMEM((N_KV_HEADS, Q_PER_KV * BLOCK, LANES), F32)],
        compiler_params=_params(),
        name="band_attn",
    )(sink, q, k, k, k, v, v, v, ctx_k, ctx_v)


def _lru_kernel(xr_ref, gx_ref, cw_ref, cb_ref, lam_ref, wg_ref, bg_ref, h0_ref, *rest, seq_len, seg, groups):
    n_cast = (len(rest) - 8) // 2
    y_ref, st_ref = rest[n_cast:n_cast + 2]
    xpad, ldec, af, bf, ab, bb = rest[2 * n_cast + 2:]
    for src, dst in zip(rest[:n_cast], rest[n_cast + 2:2 * n_cast + 2]):
        dst[...] = src[...].astype(BF16)
    t = seq_len
    n_slab = xr_ref.shape[1] // LANES
    rc = min(LRU_CHUNK, t)
    n_chunks = t // rc
    n_seg = SUBLANES * groups
    gate_slabs = min(GATE_SLABS, n_slab)
    rows = n_seg * seg

    for s in range(n_slab):
        xpad[s, 0:SUBLANES, :] = jnp.zeros((SUBLANES, LANES), F32)
        xpad[s, SUBLANES + t:2 * SUBLANES + t, :] = jnp.zeros((SUBLANES, LANES), F32)
        xpad[s, SUBLANES:SUBLANES + t, :] = xr_ref[:, s * LANES:(s + 1) * LANES]
        af[s, t:rows, :] = jnp.ones((rows - t, LANES), F32)
        ab[s, t:rows, :] = jnp.ones((rows - t, LANES), F32)
        bf[s, t:rows, :] = jnp.zeros((rows - t, LANES), F32)
        bb[s, t:rows, :] = jnp.zeros((rows - t, LANES), F32)

    ldec[...] = (-0.5 * LRU_C) * _softplus(-lam_ref[...])

    def gates_chunk(idx, carry):
        sp = idx // n_chunks
        r0 = pl.multiple_of((idx - sp * n_chunks) * rc, rc)
        base = r0 + SUBLANES - CONV_LEFT
        for k in range(gate_slabs):
            s = sp * gate_slabs + k
            cw = cw_ref[s]
            xc = cw[0:1, :] * xpad[s, pl.ds(base, rc, stride=1), :]
            for tap in range(1, CONV_W):
                xc = xc + cw[tap:tap + 1, :] * xpad[s, pl.ds(base + tap, rc, stride=1), :]
            xc = xc + cb_ref[s]
            half_x = 0.5 * xc
            gh = _dot(half_x.astype(BF16), wg_ref[s]) + 0.5 * bg_ref[s]
            hd = ldec[s]
            for d, (a_scr, b_scr) in enumerate(((af, bf), (ab, bb))):
                off = 2 * d * LANES
                th_r = jnp.tanh(gh[:, off:off + LANES])
                th_i = jnp.tanh(gh[:, off + LANES:off + 2 * LANES])
                log_a = hd[d:d + 1, :] * th_r + hd[d:d + 1, :]
                a = jnp.exp(log_a)
                u = jnp.tanh(log_a) * (-1.0 - a * a)
                root = jnp.where(u == 0.0, 0.0, u * lax.rsqrt(u))
                a_scr[s, pl.ds(r0, rc), :] = a
                b_scr[s, pl.ds(r0, rc), :] = root * (half_x * th_i + half_x)
        return carry

    lax.fori_loop(0, (n_slab // gate_slabs) * n_chunks, gates_chunk, 0)

    def seg_rows(g, i):
        return pl.ds(g * SUBLANES * seg + i, SUBLANES, stride=seg)

    ones = jnp.ones((SUBLANES, LANES), F32)
    zeros = jnp.zeros((SUBLANES, LANES), F32)
    slabs_per_pass = max(1, SCAN_CHAINS // groups)
    chains = [(k, g) for k in range(slabs_per_pass) for g in range(groups)]

    def scan_slabs(sg, carry):
        s0 = sg * slabs_per_pass

        totals = [(ones, zeros, ones, zeros) for _ in chains]
        for i in range(seg):
            ib = seg - 1 - i
            for n, (k, g) in enumerate(chains):
                pf, hf, pb, hb = totals[n]
                a = af[s0 + k, seg_rows(g, i), :]
                a2 = ab[s0 + k, seg_rows(g, ib), :]
                totals[n] = (pf * a, a * hf + bf[s0 + k, seg_rows(g, i), :],
                             pb * a2, a2 * hb + bb[s0 + k, seg_rows(g, ib), :])

        starts = {}
        for k in range(slabs_per_pass):
            c = h0_ref[s0 + k, 0:1, :]
            for g in range(groups):
                pf, hf, _, _ = totals[k * groups + g]
                fwd = []
                for j in range(SUBLANES):
                    fwd.append(c)
                    c = hf[j:j + 1] + pf[j:j + 1] * c
                starts[(k, g, 0)] = jnp.concatenate(fwd, axis=0)
            st_ref[s0 + k, 0:1, :] = c
            c = h0_ref[s0 + k, 1:2, :]
            for g in reversed(range(groups)):
                _, _, pb, hb = totals[k * groups + g]
                bwd = [None] * SUBLANES
                for j in reversed(range(SUBLANES)):
                    bwd[j] = c
                    c = hb[j:j + 1] + pb[j:j + 1] * c
                starts[(k, g, 1)] = jnp.concatenate(bwd, axis=0)
            st_ref[s0 + k, 1:2, :] = c

        state = [(starts[(k, g, 0)], starts[(k, g, 1)]) for (k, g) in chains]
        for i in range(seg):
            ib = seg - 1 - i
            for n, (k, g) in enumerate(chains):
                hf, hb = state[n]
                hf = af[s0 + k, seg_rows(g, i), :] * hf + bf[s0 + k, seg_rows(g, i), :]
                hb = ab[s0 + k, seg_rows(g, ib), :] * hb + bb[s0 + k, seg_rows(g, ib), :]
                bf[s0 + k, seg_rows(g, i), :] = hf
                bb[s0 + k, seg_rows(g, ib), :] = hb
                state[n] = (hf, hb)
        return carry

    lax.fori_loop(0, n_slab // slabs_per_pass, scan_slabs, 0)

    def out_chunk(ci, carry):
        r0 = pl.multiple_of(ci * rc, rc)
        for s in range(n_slab):
            sl = slice(s * LANES, (s + 1) * LANES)
            hsum = bf[s, pl.ds(r0, rc), :] + bb[s, pl.ds(r0, rc), :]
            y_ref[pl.ds(r0, rc), sl] = (hsum * gx_ref[pl.ds(r0, rc), sl]).astype(BF16)
        return carry

    lax.fori_loop(0, n_chunks, out_chunk, 0)


def _lru(xr, gx, h0, l, seq_len, conv_w, conv_b, lam, wg, bg, cb, groups, cast=()):
    n = xr.shape[0]
    b = n // seq_len
    n_slab = cb // LANES
    nj = D_RNN // cb
    steps = b * nj
    cast_in = [pl.BlockSpec((None, w.shape[1] // steps, w.shape[2]), lambda bi, j: (l, bi * nj + j, 0)) for w in cast]
    cast_out = [pl.BlockSpec((w.shape[1] // steps, w.shape[2]), lambda bi, j: (bi * nj + j, 0)) for w in cast]
    seg = seq_len // (SUBLANES * groups) + SEG_PAD
    assert n_slab % min(GATE_SLABS, n_slab) == 0 and n_slab % max(1, SCAN_CHAINS // groups) == 0
    rows = SUBLANES * groups * seg
    col = lambda bi, j: (bi, j)
    par = lambda r, w: pl.BlockSpec((None, n_slab, r, w), lambda bi, j: (l, j, 0, 0))
    state = pl.BlockSpec((None, n_slab, 2, LANES), lambda bi, j: (bi, j, 0, 0))
    scan_buf = pltpu.VMEM((n_slab, rows, LANES), F32)
    return pl.pallas_call(
        functools.partial(_lru_kernel, seq_len=seq_len, seg=seg, groups=groups),
        grid=(b, D_RNN // cb),
        in_specs=[
            pl.BlockSpec((seq_len, cb), col),
            pl.BlockSpec((seq_len, cb), col),
            par(CONV_W, LANES), par(1, LANES), par(2, LANES),
            par(RNN_BLOCK_W, 4 * RNN_BLOCK_W), par(1, 4 * RNN_BLOCK_W),
            state,
        ] + cast_in,
        out_specs=[pl.BlockSpec((seq_len, cb), col), state] + cast_out,
        out_shape=[
            jax.ShapeDtypeStruct((n, D_RNN), BF16),
            jax.ShapeDtypeStruct((b, N_RNN_BLOCKS, 2, LANES), F32),
        ] + [jax.ShapeDtypeStruct(w.shape[1:], BF16) for w in cast],
        scratch_shapes=[pltpu.VMEM((n_slab, seq_len + 2 * SUBLANES, LANES), F32),
                        pltpu.VMEM((n_slab, 2, LANES), F32),
                        scan_buf, scan_buf, scan_buf, scan_buf],
        compiler_params=_params(),
        name="rglru",
    )(xr, gx, conv_w, conv_b, lam, wg, bg, h0, *cast)


def _post_kernel(x_ref, attn_ref, lru_ref, g_ref, mod_ref, nw_ref, wa_ref, wl_ref, wo_ref,
                 w1_ref, w2_ref, *rest):
    n_cast = (len(rest) - 1) // 2
    o_ref = rest[n_cast]
    for src, dst in zip(rest[:n_cast], rest[n_cast + 1:]):
        dst[...] = src[...].astype(BF16)
    half = x_ref.shape[0] // ROW_SPLIT
    for r in range(ROW_SPLIT):
        rows = slice(r * half, (r + 1) * half)
        a = _dot(attn_ref[rows, :], wa_ref[...])
        b = _dot(lru_ref[rows, :], wl_ref[...])
        merged = _sigmoid(g_ref[rows, 0:D_MODEL]) * a + _sigmoid(g_ref[rows, D_MODEL:]) * b
        x = x_ref[rows, :] + mod_ref[2:3, :] * _dot(merged.astype(BF16), wo_ref[...])
        h = (_rms(x) * nw_ref[...] * (1.0 + mod_ref[4:5, :]) + mod_ref[3:4, :]).astype(BF16)
        ff = D_MODEL
        acc = None
        for c in range(D_FF // ff):
            hid = jnp.maximum(_dot(h, w1_ref[:, c * ff:(c + 1) * ff]), 0.0)
            part = _dot((hid * hid).astype(BF16), w2_ref[c * ff:(c + 1) * ff, :])
            acc = part if acc is None else acc + part
        o_ref[rows, :] = x + mod_ref[5:6, :] * acc


def _post(x, attn, lru, gates, mod, l, cond_of_tile, norm2_w, w_attn_o, w_lru_o, w_out, w1, w2, in_place, cast=()):
    n = x.shape[0]
    tm = ROW_TILE
    steps = n // tm
    row = lambda i: (i, 0)
    cast_in = [pl.BlockSpec((None, w.shape[1] // steps, w.shape[2]), lambda i: (l + 1, i, 0)) for w in cast]
    cast_out = [pl.BlockSpec((w.shape[1] // steps, w.shape[2]), row) for w in cast]
    outs = pl.pallas_call(
        _post_kernel,
        grid=(steps,),
        in_specs=[
            pl.BlockSpec((tm, D_MODEL), row),
            pl.BlockSpec((tm, ATTN_WIDTH), row),
            pl.BlockSpec((tm, D_RNN), row),
            pl.BlockSpec((tm, 2 * D_MODEL), row),
            pl.BlockSpec((None, None, N_MOD, D_MODEL), lambda i: (l, cond_of_tile(i), 0, 0)),
            _resident((None, 1, D_MODEL), lambda i: (l, 0, 0)),
            _layer_weight(w_attn_o, l), _layer_weight(w_lru_o, l), _layer_weight(w_out, l),
            _layer_weight(w1, l), _layer_weight(w2, l),
        ] + cast_in,
        out_specs=[pl.BlockSpec((tm, D_MODEL), row)] + cast_out,
        out_shape=[jax.ShapeDtypeStruct((n, D_MODEL), F32)]
        + [jax.ShapeDtypeStruct(w.shape[1:], BF16) for w in cast],
        input_output_aliases={0: 0} if in_place else {},
        compiler_params=_params(),
        name="post",
    )(x, attn, lru, gates, mod, norm2_w, w_attn_o, w_lru_o, w_out, w1, w2, *cast)
    return outs[0], outs[1:]


def _rope_tables(n_tokens):
    rows = n_tokens // GRID_W
    row = jnp.broadcast_to(jnp.arange(rows)[:, None], (rows, GRID_W)).reshape(-1).astype(F32)
    col = jnp.broadcast_to(jnp.arange(GRID_W)[None, :], (rows, GRID_W)).reshape(-1).astype(F32)
    freqs = ROPE_THETA ** (-jnp.arange(0, ROPE_AXIS_DIM, 2, dtype=F32) / ROPE_AXIS_DIM)
    ang_r = row[:, None] * freqs
    ang_c = col[:, None] * freqs
    cr, sr, cc, sc = jnp.cos(ang_r), jnp.sin(ang_r), jnp.cos(ang_c), jnp.sin(ang_c)
    return (jnp.concatenate([cr, cr, cc, cc], axis=-1),
            jnp.concatenate([-sr, sr, -sc, sc], axis=-1))


def kernel(x_prompt, x_sample, cache_k, cache_v, state_lru, c, c_ctx, ada_w, ada_b, norm1_w, norm2_w, w_in, q_norm_w, k_norm_w, attn_sink, conv_w, conv_b, lru_lambda, lru_wa, lru_ba, lru_wi, lru_bi, w_attn_o, w_lru_o, w_out, mlp_w1, mlp_w2):
    bp, tp, _ = x_prompt.shape
    bs, ts, _ = x_sample.shape
    past = cache_k.shape[2]
    assert bs + 1 <= N_COND and ts % ROW_TILE == 0 and (bp * tp) % ROW_TILE == 0

    cond = jnp.zeros((N_COND, D_MODEL), F32).at[0].set(c_ctx).at[1:1 + bs].set(c)
    mod = _adaln(cond, ada_w, ada_b).reshape(DEPTH, N_COND, N_MOD, D_MODEL)
    tiles_per_sample = ts // ROW_TILE
    cond_prompt = lambda i: 0
    cond_sample = lambda i: 1 + i // tiles_per_sample

    w_in_b = w_in[0].astype(BF16)
    post_f32 = (w_attn_o, w_lru_o, w_out, mlp_w1, mlp_w2)
    post_w = None
    wg = jnp.concatenate([lru_wa[:, 0], lru_wi[:, 0], lru_wa[:, 1], lru_wi[:, 1]], axis=-1).astype(BF16)
    blk = lambda v: v.reshape(DEPTH, N_RNN_BLOCKS, 1, RNN_BLOCK_W)
    bg = jnp.concatenate([blk(lru_ba[:, 0]), blk(lru_bi[:, 0]), blk(lru_ba[:, 1]), blk(lru_bi[:, 1])], axis=-1)
    rope_tabs = _rope_tables(ts)

    n1 = norm1_w.reshape(DEPTH, 1, D_MODEL)
    n2 = norm2_w.reshape(DEPTH, 1, D_MODEL)
    qw = q_norm_w.reshape(DEPTH, 1, HEAD_DIM)
    kw = k_norm_w.reshape(DEPTH, 1, HEAD_DIM)
    by_block = lambda v: jnp.swapaxes(v.reshape(DEPTH, -1, N_RNN_BLOCKS, RNN_BLOCK_W), 1, 2)
    cvw = by_block(conv_w)
    cvb = by_block(conv_b)
    lam = by_block(lru_lambda)
    state_by_block = lambda v: jnp.swapaxes(v.reshape(-1, 2, N_RNN_BLOCKS, RNN_BLOCK_W), 1, 2)
    zero_state = jnp.zeros((bp, N_RNN_BLOCKS, 2, RNN_BLOCK_W), F32)

    xp = x_prompt.reshape(bp * tp, D_MODEL)
    xs = x_sample.reshape(bs * ts, D_MODEL)
    kv_cache, new_s = None, []
    for l in range(DEPTH):
        in_place = l > 0
        q, k, v, xr, gx, gates, *kv_cache = _inproj(xp, mod, l, cond_prompt, n1, w_in_b, qw, kw, kv_cache=kv_cache,
                                                    kv_seq_len=tp, tm=ROW_TILE if kv_cache else tp)
        attn = _ctx_attn(attn_sink[l], q.reshape(bp, tp, -1), k.reshape(bp, tp, -1), v.reshape(bp, tp, -1))
        lru, st, *first_w = _lru(xr, gx, zero_state, l, tp, cvw, cvb, lam, wg, bg, *LRU_CTX,
                                 cast=post_f32 if post_w is None else ())
        post_w = post_w or first_w
        more = l + 1 < DEPTH
        xp, next_w_in = _post(xp, attn.reshape(bp * tp, -1), lru, gates, mod, l, cond_prompt, n2,
                              *post_w, in_place, cast=(w_in,) if more else ())
        new_s.append(jnp.swapaxes(st, 1, 2).reshape(bp, 2, D_RNN))
        q, k, v, xr, gx, gates = _inproj(xs, mod, l, cond_sample, n1, w_in_b, qw, kw, rope_tabs=rope_tabs)
        attn = _band_attn(attn_sink[l], q.reshape(bs, ts, -1), k.reshape(bs, ts, -1),
                          v.reshape(bs, ts, -1), cache_k, cache_v, l)
        lru, _ = _lru(xr, gx, state_by_block(state_lru[:, l]), l, ts, cvw, cvb, lam, wg, bg, *LRU_LATENT)[:2]
        xs, next_post_w = _post(xs, attn.reshape(bs * ts, -1), lru, gates, mod, l, cond_sample, n2,
                                *post_w, in_place,
                                cast=post_f32 if more else ())
        if more:
            (w_in_b,), post_w = next_w_in, next_post_w

    new_k_arr, new_v_arr = kv_cache
    return (xp.reshape(bp, tp, D_MODEL), xs.reshape(bs, ts, D_MODEL), new_k_arr, new_v_arr,
            jnp.stack(new_s, axis=1))
```
